```python
import math
import jax, jax.numpy as jnp
from jax import lax
import numpy as np

D_MODEL = 2048
BATCH = 4
SEQ = 2048
DEPTH = 4
DEC_BATCH = 128
DEC_SEQ = 8
PAST_LEN = 16384
PAGE_SIZE = 128

N_EVEN = (DEPTH + 1) // 2
N_ODD = DEPTH // 2
EPS = 1e-6
D_RG = D_MODEL // 2
RG_BLOCKS = 16
RG_BW = D_RG // RG_BLOCKS
RG_C = 8.0
CONV_K = 4
ML_HEADS = 4
ML_DK = D_MODEL // 16
ML_DV = D_MODEL // 8
ML_CHUNK = 64
ML_HK = ML_HEADS * ML_DK
ML_HV = ML_HEADS * ML_DV
CONV_W = D_RG + 2 * ML_HK
E_IN = CONV_W + D_RG + 2 * ML_HV + 2 * ML_HEADS
D_C = D_MODEL
C_GROUPS = 8
C_GW = D_C // C_GROUPS
C_CHUNK = 128
D_FF = 5632
FFN_K = 3

kernel_name = 'hybrid_rglru_mlstm_chunkmlp_step'


def rmsnorm(x, g):
    xf = x.astype(jnp.float32)
    y = xf * lax.rsqrt(jnp.mean(xf * xf, axis=-1, keepdims=True) + EPS)
    return (y * g.astype(jnp.float32)).astype(x.dtype)


def causal_dwconv(x, buf, w, b):
    S = x.shape[1]
    xp = jnp.concatenate([buf.astype(x.dtype), x], axis=1)
    y = b.astype(x.dtype)
    for k in range(w.shape[0]):
        y = y + w[k].astype(x.dtype) * xp[:, k:k + S]
    return y, xp[:, S:]


def rglru(xc, h0, wa, ba, wx, bx, lam):
    B, S, _ = xc.shape
    xb = xc.reshape(B, S, RG_BLOCKS, RG_BW)
    r = jax.nn.sigmoid(jnp.einsum('bsnc,ncd->bsnd', xb, wa.astype(jnp.float32)).reshape(B, S, D_RG) + ba)
    i = jax.nn.sigmoid(jnp.einsum('bsnc,ncd->bsnd', xb, wx.astype(jnp.float32)).reshape(B, S, D_RG) + bx)
    log_a = -RG_C * r * jax.nn.softplus(-lam.astype(jnp.float32))
    u = jnp.sqrt(-jnp.expm1(2.0 * log_a)) * (i * xc)

    def combine(e, l):
        return e[0] * l[0], l[0] * e[1] + l[1]

    a_cum, b_cum = lax.associative_scan(combine, (jnp.exp(log_a), u), axis=1)
    h = a_cum * h0[:, None, :] + b_cum
    return h, h[:, -1]


def mlstm(q, k, v, logi, logf, C0, n0, m0):
    B, S, H, _ = q.shape
    L = math.gcd(ML_CHUNK, S)
    NC = S // L

    def chunks(t):
        return jnp.moveaxis(t.reshape((B, NC, L) + t.shape[2:]), 1, 0)

    causal = jnp.tril(jnp.ones((L, L), dtype=bool))

    def step(carry, xs):
        C, n, m = carry
        qc, kc, vc, ic, fc = xs
        b = jnp.cumsum(fc, axis=1).transpose(0, 2, 1)
        ic = ic.transpose(0, 2, 1)
        D = b[..., :, None] - b[..., None, :] + ic[..., None, :]
        D = jnp.where(causal, D, -jnp.inf)
        inter = b + m[..., None]
        m_t = jnp.maximum(inter, jnp.max(D, axis=-1))
        s = jnp.einsum('blhd,bshd->bhls', qc, kc) * jnp.exp(D - m_t[..., None])
        w_inter = jnp.exp(inter - m_t)
        num = w_inter[..., None] * jnp.einsum('blhd,bhde->bhle', qc, C) + jnp.einsum('bhls,bshe->bhle', s, vc)
        den = w_inter * jnp.einsum('blhd,bhd->bhl', qc, n) + jnp.sum(s, axis=-1)
        h = num / jnp.maximum(jnp.abs(den), jnp.exp(-m_t))[..., None]
        m_new = m_t[..., -1]
        w_state = jnp.exp(b[..., -1:] - b + ic - m_new[..., None])
        decay = jnp.exp(b[..., -1] + m - m_new)
        C_new = decay[..., None, None] * C + jnp.einsum('bhs,bshd,bshe->bhde', w_state, kc, vc)
        n_new = decay[..., None] * n + jnp.einsum('bhs,bshd->bhd', w_state, kc)
        return (C_new, n_new, m_new), h.transpose(0, 2, 1, 3)

    (C, n, m), hs = lax.scan(step, (C0, n0, m0),
                             (chunks(q), chunks(k), chunks(v), chunks(logi), chunks(logf)))
    h = jnp.moveaxis(hs, 0, 1).reshape(B, S, H, ML_DV)
    return h, C, n, m


def even_mixer(xn, conv_buf, h0, C0, n0, m0, P, j):
    f32 = jnp.float32
    B, S, _ = xn.shape
    proj = (xn @ P['w_in_even'][j]).astype(f32)
    o1 = CONV_W
    o2 = o1 + D_RG
    o3 = o2 + ML_HV
    o4 = o3 + ML_HV
    conv_out, conv_new = causal_dwconv(proj[..., :o1], conv_buf, P['conv_even_w'][j], P['conv_even_b'][j])
    h, h_last = rglru(conv_out[..., :D_RG], h0.astype(f32), P['rg_wa'][j], P['rg_ba'][j],
                      P['rg_wx'][j], P['rg_bx'][j], P['rg_lambda'][j])
    rg_out = jax.nn.gelu(proj[..., o1:o2]) * h
    qk = jax.nn.silu(conv_out[..., D_RG:])
    q = qk[..., :ML_HK].reshape(B, S, ML_HEADS, ML_DK)
    k = qk[..., ML_HK:].reshape(B, S, ML_HEADS, ML_DK) * (ML_DK ** -0.5)
    v = proj[..., o2:o3].reshape(B, S, ML_HEADS, ML_DV)
    o_gate = jax.nn.sigmoid(proj[..., o3:o4])
    gates = proj[..., o4:].reshape(B, S, 2, ML_HEADS) + P['ml_gate_b'][j].astype(f32)
    hm, C, n, m = mlstm(q, k, v, gates[..., 0, :], jax.nn.log_sigmoid(gates[..., 1, :]),
                        C0.astype(f32), n0.astype(f32), m0.astype(f32))
    hm = hm * lax.rsqrt(jnp.mean(hm * hm, axis=-1, keepdims=True) + EPS) * P['ml_norm_g'][j].astype(f32)
    ml_out = hm.reshape(B, S, ML_HV) * o_gate
    mixed = jnp.concatenate([rg_out, ml_out], axis=-1).astype(xn.dtype)
    return mixed @ P['w_out_even'][j], (conv_new, h_last, C, n, m)


def odd_mixer(xn, P, j):
    f32 = jnp.float32
    B, S, _ = xn.shape
    proj = jax.nn.gelu((xn @ P['w_in_odd'][j]).astype(f32))
    u, v = proj[..., :D_C], proj[..., D_C:]
    mu = jnp.mean(v, axis=-1, keepdims=True)
    var = jnp.mean(jnp.square(v - mu), axis=-1, keepdims=True)
    v = (v - mu) * lax.rsqrt(var + EPS) * P['sgu_ln_g'][j].astype(f32) + P['sgu_ln_b'][j].astype(f32)
    Sp = -(-S // C_CHUNK) * C_CHUNK
    vp = jnp.pad(v, ((0, 0), (0, Sp - S), (0, 0))).reshape(B, Sp // C_CHUNK, C_CHUNK, C_GROUPS, C_GW)
    ws = jnp.tril(P['sgu_ws'][j].astype(f32))
    mix = jnp.einsum('gts,bnsgc->bntgc', ws, vp) + P['sgu_b'][j].astype(f32).T[:, :, None]
    mix = mix.reshape(B, Sp, D_C)[:, :S]
    out = (u * mix).astype(xn.dtype) @ P['w_out_odd'][j]
    return out, v


def conv_ffn(xn, buf, P, layer):
    up = xn @ P['ffn_w_up'][layer]
    g, u = up[..., :D_FF], up[..., D_FF:]
    gc, buf_new = causal_dwconv(g.astype(jnp.float32), buf, P['ffn_conv_w'][layer], P['ffn_conv_b'][layer])
    hid = (jax.nn.gelu(gc) * u.astype(jnp.float32)).astype(xn.dtype)
    return hid @ P['ffn_w_down'][layer], buf_new


def trunk(x, conv_buf, rg_h, ml_C, ml_n, ml_m, ffn_buf, P):
    conv_l, h_l, C_l, n_l, m_l, v_l, f_l = [], [], [], [], [], [], []
    for layer in range(DEPTH):
        j = layer // 2
        xn = rmsnorm(x, P['norm_mix'][layer])
        if layer % 2 == 0:
            mix, (cb, hl, C, n, m) = even_mixer(xn, conv_buf[j], rg_h[j], ml_C[j], ml_n[j], ml_m[j], P, j)
            conv_l.append(cb)
            h_l.append(hl)
            C_l.append(C)
            n_l.append(n)
            m_l.append(m)
        else:
            mix, v = odd_mixer(xn, P, j)
            v_l.append(v)
        x = x + mix.astype(x.dtype)
        f, fb = conv_ffn(rmsnorm(x, P['norm_ffn'][layer]), ffn_buf[layer], P, layer)
        f_l.append(fb)
        x = x + f.astype(x.dtype)
    y = rmsnorm(x, P['norm_final'])
    dt = x.dtype
    return (y, jnp.stack(conv_l).astype(dt), jnp.stack(h_l).astype(dt), jnp.stack(C_l).astype(dt),
            jnp.stack(n_l).astype(dt), jnp.stack(m_l).astype(dt), v_l, jnp.stack(f_l).astype(dt))


def setup_inputs(seed: int = 0) -> dict:
    key = jax.random.key(seed)
    ks = iter(jax.random.split(key, 40))

    def nrm(shape, s):
        return s * jax.random.normal(next(ks), shape, jnp.float32)

    a0 = jax.random.uniform(next(ks), (N_EVEN, D_RG), jnp.float32, 0.9, 0.999) ** (1.0 / RG_C)
    lam = jnp.log(a0) - jnp.log1p(-a0)
    i_b = nrm((N_EVEN, ML_HEADS), 0.1)
    f_b = jnp.linspace(3.0, 6.0, ML_HEADS, dtype=jnp.float32) + nrm((N_EVEN, ML_HEADS), 0.1)
    ml_gate_b = jnp.stack([i_b, f_b], axis=1)
    return {
        'x_prompt': nrm((BATCH, SEQ, D_MODEL), 1.0),
        'x_sample': nrm((DEC_BATCH, DEC_SEQ, D_MODEL), 1.0),
        'state_conv_mix': nrm((N_EVEN, DEC_BATCH, CONV_K - 1, CONV_W), 1.0),
        'state_rglru_h': nrm((N_EVEN, DEC_BATCH, D_RG), 0.5),
        'state_mlstm_C': nrm((N_EVEN, DEC_BATCH, ML_HEADS, ML_DK, ML_DV), 0.1),
        'state_mlstm_n': nrm((N_EVEN, DEC_BATCH, ML_HEADS, ML_DK), 0.1),
        'state_mlstm_m': nrm((N_EVEN, DEC_BATCH, ML_HEADS), 0.5),
        'state_ffn_conv': nrm((DEPTH, DEC_BATCH, FFN_K - 1, D_FF), 1.0),
        'norm_mix': 1.0 + nrm((DEPTH, D_MODEL), 0.02),
        'norm_ffn': 1.0 + nrm((DEPTH, D_MODEL), 0.02),
        'norm_final': 1.0 + nrm((D_MODEL,), 0.02),
        'w_in_even': nrm((N_EVEN, D_MODEL, E_IN), D_MODEL ** -0.5),
        'conv_even_w': nrm((N_EVEN, CONV_K, CONV_W), 0.5),
        'conv_even_b': nrm((N_EVEN, CONV_W), 0.01),
        'rg_wa': nrm((N_EVEN, RG_BLOCKS, RG_BW, RG_BW), RG_BW ** -0.5),
        'rg_ba': nrm((N_EVEN, D_RG), 0.01),
        'rg_wx': nrm((N_EVEN, RG_BLOCKS, RG_BW, RG_BW), RG_BW ** -0.5),
        'rg_bx': nrm((N_EVEN, D_RG), 0.01),
        'rg_lambda': lam,
        'ml_gate_b': ml_gate_b,
        'ml_norm_g': 1.0 + nrm((N_EVEN, ML_HEADS, ML_DV), 0.02),
        'w_out_even': nrm((N_EVEN, D_RG + ML_HV, D_MODEL), 0.5 * (D_RG + ML_HV) ** -0.5),
        'w_in_odd': nrm((N_ODD, D_MODEL, 2 * D_C), D_MODEL ** -0.5),
        'sgu_ln_g': 1.0 + nrm((N_ODD, D_C), 0.02),
        'sgu_ln_b': nrm((N_ODD, D_C), 0.01),
        'sgu_ws': nrm((N_ODD, C_GROUPS, C_CHUNK, C_CHUNK), C_CHUNK ** -0.5),
        'sgu_b': 1.0 + nrm((N_ODD, C_GROUPS, C_CHUNK), 0.1),
        'w_out_odd': nrm((N_ODD, D_C, D_MODEL), 0.5 * D_C ** -0.5),
        'ffn_w_up': nrm((DEPTH, D_MODEL, 2 * D_FF), D_MODEL ** -0.5),
        'ffn_conv_w': nrm((DEPTH, FFN_K, D_FF), FFN_K ** -0.5),
        'ffn_conv_b': nrm((DEPTH, D_FF), 0.01),
        'ffn_w_down': nrm((DEPTH, D_FF, D_MODEL), 0.5 * D_FF ** -0.5),
    }


def reference(x_prompt, x_sample, state_conv_mix, state_rglru_h, state_mlstm_C, state_mlstm_n,
              state_mlstm_m, state_ffn_conv, norm_mix, norm_ffn, norm_final, w_in_even,
              conv_even_w, conv_even_b, rg_wa, rg_ba, rg_wx, rg_bx, rg_lambda, ml_gate_b,
              ml_norm_g, w_out_even, w_in_odd, sgu_ln_g, sgu_ln_b, sgu_ws, sgu_b, w_out_odd,
              ffn_w_up, ffn_conv_w, ffn_conv_b, ffn_w_down):
    P = {
        'norm_mix': norm_mix, 'norm_ffn': norm_ffn, 'norm_final': norm_final,
        'w_in_even': w_in_even, 'conv_even_w': conv_even_w, 'conv_even_b': conv_even_b,
        'rg_wa': rg_wa, 'rg_ba': rg_ba, 'rg_wx': rg_wx, 'rg_bx': rg_bx, 'rg_lambda': rg_lambda,
        'ml_gate_b': ml_gate_b, 'ml_norm_g': ml_norm_g, 'w_out_even': w_out_even,
        'w_in_odd': w_in_odd, 'sgu_ln_g': sgu_ln_g, 'sgu_ln_b': sgu_ln_b, 'sgu_ws': sgu_ws,
        'sgu_b': sgu_b, 'w_out_odd': w_out_odd, 'ffn_w_up': ffn_w_up, 'ffn_conv_w': ffn_conv_w,
        'ffn_conv_b': ffn_conv_b, 'ffn_w_down': ffn_w_down,
    }
    Bp = x_prompt.shape[0]
    dt = x_prompt.dtype
    z_conv = jnp.zeros((N_EVEN, Bp, CONV_K - 1, CONV_W), dt)
    z_h = jnp.zeros((N_EVEN, Bp, D_RG), dt)
    z_C = jnp.zeros((N_EVEN, Bp, ML_HEADS, ML_DK, ML_DV), dt)
    z_n = jnp.zeros((N_EVEN, Bp, ML_HEADS, ML_DK), dt)
    z_m = jnp.zeros((N_EVEN, Bp, ML_HEADS), dt)
    z_f = jnp.zeros((DEPTH, Bp, FFN_K - 1, D_FF), dt)
    y_prompt, conv_p, h_p, C_p, n_p, m_p, _, f_p = trunk(x_prompt, z_conv, z_h, z_C, z_n, z_m, z_f, P)
    y_sample, conv_s, h_s, C_s, n_s, m_s, v_list, f_s = trunk(
        x_sample, state_conv_mix, state_rglru_h, state_mlstm_C, state_mlstm_n, state_mlstm_m,
        state_ffn_conv, P)
    chunk_v_s = jnp.stack(v_list).astype(x_sample.dtype)
    return (y_prompt, y_sample, conv_p, conv_s, h_p, h_s, C_p, C_s, n_p, n_s, m_p, m_s,
            chunk_v_s, f_p, f_s)
```

```python
import functools

import jax
import jax.numpy as jnp
from jax import lax
from jax.experimental import pallas as pl
from jax.experimental.pallas import tpu as pltpu

F32 = jnp.float32
BF16 = jnp.bfloat16

D_MODEL = 2048
EPS = 1e-6
D_RG = D_MODEL // 2
RG_BLOCKS = 16
RG_BW = D_RG // RG_BLOCKS
RG_C = 8.0
CONV_K = 4
ML_HEADS = 4
ML_DK = D_MODEL // 16
ML_DV = D_MODEL // 8
ML_HK = ML_HEADS * ML_DK
ML_HV = ML_HEADS * ML_DV
CONV_W = D_RG + 2 * ML_HK
E_MAIN = CONV_W + D_RG + 2 * ML_HV
D_C = D_MODEL
C_GROUPS = 8
C_GW = D_C // C_GROUPS
C_CHUNK = 128
D_FF = 5632
FFN_K = 3

SUBLANES = 8
LANES = 128
HALO = SUBLANES
RG_CB = 256
VMEM_LIMIT = 56 * 1024 * 1024


def _params(*sem):
    return pltpu.CompilerParams(dimension_semantics=sem, vmem_limit_bytes=VMEM_LIMIT)


def _rmsnorm_kernel(x_ref, g_ref, o_ref):
    x = x_ref[...]
    ms = jnp.mean(x * x, axis=-1, keepdims=True)
    o_ref[...] = (x * lax.rsqrt(ms + EPS) * g_ref[...]).astype(o_ref.dtype)


def rmsnorm(x, g, out_dtype, tm=512):
    m, d = x.shape
    return pl.pallas_call(
        _rmsnorm_kernel,
        grid=(m // tm,),
        in_specs=[pl.BlockSpec((tm, d), lambda i: (i, 0)), pl.BlockSpec((1, d), lambda i: (0, 0))],
        out_specs=pl.BlockSpec((tm, d), lambda i: (i, 0)),
        out_shape=jax.ShapeDtypeStruct((m, d), out_dtype),
        compiler_params=_params("parallel"),
        name="rmsnorm",
    )(x, g.reshape(1, d))


def _mm_kernel(*refs, ka, nk, has_res, act):
    n_a = len(ka)
    a_refs, w_ref = refs[:n_a], refs[n_a]
    res_ref = refs[n_a + 1] if has_res else None
    o_ref = refs[n_a + 1 + int(has_res)]
    part = None
    off = 0
    for a_ref, k in zip(a_refs, ka):
        d = jnp.dot(a_ref[...], w_ref[off:off + k, :], preferred_element_type=F32)
        part = d if part is None else part + d
        off += k

    def finish(acc):
        if act == "gelu":
            acc = jax.nn.gelu(acc)
        if has_res:
            acc = res_ref[...] + acc
        o_ref[...] = acc.astype(o_ref.dtype)

    if nk == 1:
        finish(part)
    else:
        acc_ref = refs[-1]
        kk = pl.program_id(2)

        @pl.when(kk == 0)
        def _():
            acc_ref[...] = part

        @pl.when(kk > 0)
        def _():
            acc_ref[...] += part

        @pl.when(kk == nk - 1)
        def _():
            finish(acc_ref[...])


def matmul(a_list, w, res=None, act=None, out_dtype=F32, tm=512, tn=1024, tk=None):
    m = a_list[0].shape[0]
    ka = tuple(a.shape[1] for a in a_list)
    kt, n = w.shape
    tm, tn = min(tm, m), min(tn, n)
    if tk is None or len(a_list) > 1:
        tk = kt
    nk = kt // tk
    in_specs = []
    if nk == 1:
        for k in ka:
            in_specs.append(pl.BlockSpec((tm, k), lambda i, j, kk: (i, 0)))
    else:
        in_specs.append(pl.BlockSpec((tm, tk), lambda i, j, kk: (i, kk)))
    in_specs.append(pl.BlockSpec((tk, tn), lambda i, j, kk: (kk, j)))
    args = list(a_list) + [w]
    if res is not None:
        in_specs.append(pl.BlockSpec((tm, tn), lambda i, j, kk: (i, j)))
        args.append(res)
    scratch = [pltpu.VMEM((tm, tn), F32)] if nk > 1 else []
    return pl.pallas_call(
        functools.partial(_mm_kernel, ka=ka if nk == 1 else (tk,), nk=nk, has_res=res is not None, act=act),
        grid=(m // tm, n // tn, nk),
        in_specs=in_specs,
        out_specs=pl.BlockSpec((tm, tn), lambda i, j, kk: (i, j)),
        out_shape=jax.ShapeDtypeStruct((m, n), out_dtype),
        scratch_shapes=scratch,
        compiler_params=_params("parallel", "parallel", "arbitrary"),
        name="matmul",
    )(*args)


def _causal_conv(xs_ref, x3, buf_ref, w_ref, b_ref, first, rows, taps, carry):
    @pl.when(first)
    def _():
        xs_ref[:, 0:HALO, :] = buf_ref[...]

    xs_ref[:, HALO:HALO + rows, :] = x3
    y = b_ref[...] + w_ref[taps - 1:taps, :] * x3
    for k in range(taps - 1):
        lo = HALO - (taps - 1) + k
        y = y + w_ref[k:k + 1, :] * xs_ref[:, lo:lo + rows, :]
    if carry:
        xs_ref[:, 0:HALO, :] = xs_ref[:, rows:rows + HALO, :]
    return y


def _ffn_up_kernel(a_ref, wg_ref, wu_ref, cw_ref, cb_ref, buf_ref, hid_ref, st_ref, gs_ref, *, ns, rows, tps):
    a = a_ref[...]
    tn = wg_ref.shape[1]
    g3 = jnp.dot(a, wg_ref[...], preferred_element_type=F32).reshape(ns, rows, tn)
    u3 = jnp.dot(a, wu_ref[...], preferred_element_type=F32).reshape(ns, rows, tn)
    first = (pl.program_id(1) % tps) == 0
    gc = _causal_conv(gs_ref, g3, buf_ref, cw_ref, cb_ref, first, rows, FFN_K, tps > 1)
    hid_ref[...] = (jax.nn.gelu(gc) * u3).reshape(ns * rows, tn).astype(hid_ref.dtype)
    st_ref[...] = g3[:, rows - (FFN_K - 1):rows, :]


def ffn_up(xn, wg, wu, cw, cb, bufp, seq, tm=512, tn=512):
    m, d = xn.shape
    nb = m // seq
    rows = min(seq, tm)
    ns = tm // rows
    tps = seq // rows
    kern = functools.partial(_ffn_up_kernel, ns=ns, rows=rows, tps=tps)
    return pl.pallas_call(
        kern,
        grid=(D_FF // tn, m // tm),
        in_specs=[
            pl.BlockSpec((tm, d), lambda j, i: (i, 0)),
            pl.BlockSpec((d, tn), lambda j, i: (0, j)),
            pl.BlockSpec((d, tn), lambda j, i: (0, j)),
            pl.BlockSpec((FFN_K, tn), lambda j, i: (0, j)),
            pl.BlockSpec((1, tn), lambda j, i: (0, j)),
            pl.BlockSpec((ns, HALO, tn), lambda j, i: (i // tps, 0, j)),
        ],
        out_specs=[
            pl.BlockSpec((tm, tn), lambda j, i: (i, j)),
            pl.BlockSpec((ns, FFN_K - 1, tn), lambda j, i: (i // tps, 0, j)),
        ],
        out_shape=[
            jax.ShapeDtypeStruct((m, D_FF), BF16),
            jax.ShapeDtypeStruct((nb, FFN_K - 1, D_FF), F32),
        ],
        scratch_shapes=[pltpu.VMEM((ns, HALO + rows, tn), F32)],
        compiler_params=_params("parallel", "arbitrary"),
        name="ffn_up",
    )(xn, wg, wu, cw, cb, bufp)


def _seq_scan(a, b, rows):
    pos = lax.broadcasted_iota(jnp.int32, a.shape, 0) & (rows - 1)
    d = 1
    while d < rows:
        m = pos >= d
        a_sh = jnp.where(m, pltpu.roll(a, d, 0), 1.0)
        b_sh = jnp.where(m, pltpu.roll(b, d, 0), 0.0)
        b = a * b_sh + b
        a = a * a_sh
        d *= 2
    return a, b


def _rglru_kernel(x_ref, gate_ref, cw_ref, cb_ref, wa_ref, wx_ref, ba_ref, bx_ref, lam_ref, buf_ref, h0_ref,
                  out_ref, cst_ref, hl_ref, xs_ref, *, ns, rows, tps):
    c = x_ref.shape[1]
    first = pl.program_id(2) == 0

    @pl.when(first)
    def _():
        hl_ref[...] = h0_ref[...]

    x3 = x_ref[...].reshape(ns, rows, c)
    xc = _causal_conv(xs_ref, x3, buf_ref, cw_ref, cb_ref, first, rows, CONV_K, tps > 1)
    cst_ref[...] = x3[:, rows - (CONV_K - 1):rows, :]
    xc = xc.reshape(ns * rows, c)
    xb = xc.astype(BF16)
    r = jax.nn.sigmoid(jnp.dot(xb, wa_ref[0], preferred_element_type=F32) + ba_ref[...])
    i = jax.nn.sigmoid(jnp.dot(xb, wx_ref[0], preferred_element_type=F32) + bx_ref[...])
    nl = -lam_ref[...]
    softplus = jnp.maximum(nl, 0.0) + jnp.log1p(jnp.exp(-jnp.abs(nl)))
    log_a = (-RG_C) * r * softplus
    th = jnp.tanh(log_a)
    u = jnp.sqrt(-2.0 * th / (1.0 - th)) * (i * xc)
    a_cum, b_cum = _seq_scan(jnp.exp(log_a), u, rows)
    h = a_cum.reshape(ns, rows, c) * hl_ref[...] + b_cum.reshape(ns, rows, c)
    hl_ref[...] = h[:, rows - 1:rows, :]
    out_ref[...] = (jax.nn.gelu(gate_ref[...]) * h.reshape(ns * rows, c)).astype(out_ref.dtype)


def rglru(proj, cw, cb, wa_bd, wx_bd, ba, bx, lam, bufp, h0, seq, rows, ns):
    m = proj.shape[0]
    nb = m // seq
    tps = seq // rows
    tm = ns * rows
    c = RG_CB
    gate0 = CONV_W // c
    kern = functools.partial(_rglru_kernel, ns=ns, rows=rows, tps=tps)
    vec = pl.BlockSpec((1, c), lambda b, cb_, t: (0, cb_))
    return pl.pallas_call(
        kern,
        grid=(nb // ns, D_RG // c, tps),
        in_specs=[
            pl.BlockSpec((tm, c), lambda b, cb_, t: (b * tps + t, cb_)),
            pl.BlockSpec((tm, c), lambda b, cb_, t: (b * tps + t, gate0 + cb_)),
            pl.BlockSpec((CONV_K, c), lambda b, cb_, t: (0, cb_)),
            vec,
            pl.BlockSpec((1, c, c), lambda b, cb_, t: (cb_, 0, 0)),
            pl.BlockSpec((1, c, c), lambda b, cb_, t: (cb_, 0, 0)),
            vec, vec, vec,
            pl.BlockSpec((ns, HALO, c), lambda b, cb_, t: (b, 0, cb_)),
            pl.BlockSpec((ns, 1, c), lambda b, cb_, t: (b, 0, cb_)),
        ],
        out_specs=[
            pl.BlockSpec((tm, c), lambda b, cb_, t: (b * tps + t, cb_)),
            pl.BlockSpec((ns, CONV_K - 1, c), lambda b, cb_, t: (b, 0, cb_)),
            pl.BlockSpec((ns, 1, c), lambda b, cb_, t: (b, 0, cb_)),
        ],
        out_shape=[
            jax.ShapeDtypeStruct((m, D_RG), BF16),
            jax.ShapeDtypeStruct((nb, CONV_K - 1, D_RG), F32),
            jax.ShapeDtypeStruct((nb, 1, D_RG), F32),
        ],
        scratch_shapes=[pltpu.VMEM((ns, HALO + rows, c), F32)],
        compiler_params=_params("parallel", "parallel", "arbitrary"),
        name="rglru",
    )(proj, proj, cw, cb, wa_bd, wx_bd, ba, bx, lam, bufp, h0)


def _heads(x3, width, ns, rows):
    parts = [x3[:, None, :, h * width:(h + 1) * width] for h in range(ML_HEADS)]
    return jnp.concatenate(parts, axis=1).reshape(ns * ML_HEADS, rows, width)


def _mlstm_kernel(qk_ref, v_ref, o_ref, g_ref, cw_ref, cb_ref, gb_ref, ng_ref, buf_ref, c0_ref, n0_ref, m0_ref,
                  out_ref, cst_ref, c_ref, n_ref, m_ref, xs_ref, *, ns, rows, tps):
    G = ns * ML_HEADS
    first = pl.program_id(1) == 0

    @pl.when(first)
    def _():
        c_ref[...] = c0_ref[...]
        n_ref[...] = n0_ref[...]
        m_ref[...] = m0_ref[...]

    x3 = qk_ref[...].reshape(ns, rows, 2 * ML_HK)
    xc = _causal_conv(xs_ref, x3, buf_ref, cw_ref, cb_ref, first, rows, CONV_K, tps > 1)
    cst_ref[...] = x3[:, rows - (CONV_K - 1):rows, :]
    qk = jax.nn.silu(xc)
    q = _heads(qk[:, :, :ML_HK], ML_DK, ns, rows)
    k = _heads(qk[:, :, ML_HK:], ML_DK, ns, rows) * (ML_DK ** -0.5)
    v = _heads(v_ref[...].reshape(ns, rows, ML_HV), ML_DV, ns, rows)
    gates = g_ref[...].reshape(ns, rows, LANES) + gb_ref[...]
    i_col = _heads(gates[:, :, :ML_HEADS], 1, ns, rows)
    f_col = _heads(jax.nn.log_sigmoid(gates[:, :, ML_HEADS:2 * ML_HEADS]), 1, ns, rows)

    t_idx = lax.broadcasted_iota(jnp.int32, (rows, rows), 0)
    s_idx = lax.broadcasted_iota(jnp.int32, (rows, rows), 1)
    eye = t_idx == s_idx
    causal = t_idx >= s_idx
    i_row = jnp.sum(jnp.where(eye, i_col, 0.0), axis=1, keepdims=True)
    f_row = jnp.sum(jnp.where(eye, f_col, 0.0), axis=1, keepdims=True)
    b_col = jnp.sum(jnp.where(causal, f_row, 0.0), axis=2, keepdims=True)
    b_row = jnp.sum(jnp.where(t_idx <= s_idx, f_col, 0.0), axis=1, keepdims=True)
    dmat = jnp.where(causal, b_col - b_row + i_row, -jnp.inf)

    m_prev = m_ref[...].reshape(G, 1, LANES)[:, :, 0:1]
    inter = b_col + m_prev
    m_t = jnp.maximum(inter, jnp.max(dmat, axis=2, keepdims=True))
    qb, kb, vb = q.astype(BF16), k.astype(BF16), v.astype(BF16)
    s = jnp.einsum("gld,gsd->gls", qb, kb, preferred_element_type=F32) * jnp.exp(dmat - m_t)
    w_inter = jnp.exp(inter - m_t)
    c_old = c_ref[...].reshape(G, ML_DK, ML_DV)
    n_old = n_ref[...].reshape(G, 1, ML_DK)
    num = (w_inter * jnp.einsum("gld,gde->gle", qb, c_old.astype(BF16), preferred_element_type=F32)
           + jnp.einsum("gls,gse->gle", s.astype(BF16), vb, preferred_element_type=F32))
    den = w_inter * jnp.sum(q * n_old, axis=2, keepdims=True) + jnp.sum(s, axis=2, keepdims=True)
    h = num / jnp.maximum(jnp.abs(den), jnp.exp(-m_t))

    m_new = m_t[:, rows - 1:rows, :]
    b_last = b_col[:, rows - 1:rows, :]
    w_state = jnp.exp(b_last - b_col + i_col - m_new)
    decay = jnp.exp(b_last + m_prev - m_new)
    kw = k * w_state
    kwt = jnp.swapaxes(kw, 1, 2).astype(BF16)
    c_new = decay * c_old + jnp.einsum("gds,gse->gde", kwt, vb, preferred_element_type=F32)
    n_new = decay * n_old + jnp.sum(kw, axis=1, keepdims=True)
    c_ref[...] = c_new.reshape(ns, ML_HEADS, ML_DK, ML_DV)
    n_ref[...] = n_new.reshape(ns, ML_HEADS, 1, ML_DK)
    m_ref[...] = jnp.broadcast_to(m_new, (G, 1, LANES)).reshape(ns, ML_HEADS, 1, LANES)

    hn = h * lax.rsqrt(jnp.mean(h * h, axis=2, keepdims=True) + EPS)
    hn = hn.reshape(ns, ML_HEADS, rows, ML_DV)
    o3 = o_ref[...].reshape(ns, rows, ML_HV)
    for hh in range(ML_HEADS):
        sl = slice(hh * ML_DV, (hh + 1) * ML_DV)
        y = hn[:, hh] * ng_ref[:, sl] * jax.nn.sigmoid(o3[:, :, sl])
        out_ref[:, sl] = y.reshape(ns * rows, ML_DV).astype(out_ref.dtype)


def mlstm(proj, gates, cw, cb, gbias, ng, bufp, c0, n0, m0, seq, rows, ns):
    m = proj.shape[0]
    nb = m // seq
    tps = seq // rows
    tm = ns * rows
    w = 2 * ML_HK
    kern = functools.partial(_mlstm_kernel, ns=ns, rows=rows, tps=tps)
    row_blk = lambda col: pl.BlockSpec((tm, w), lambda b, t: (b * tps + t, col))
    st4 = lambda *shape: pl.BlockSpec((ns,) + shape, lambda b, t: (b, 0, 0, 0))
    return pl.pallas_call(
        kern,
        grid=(nb // ns, tps),
        in_specs=[
            row_blk(D_RG // w),
            row_blk((CONV_W + D_RG) // w),
            row_blk((CONV_W + D_RG + ML_HV) // w),
            pl.BlockSpec((tm, LANES), lambda b, t: (b * tps + t, 0)),
            pl.BlockSpec((CONV_K, w), lambda b, t: (0, D_RG // w)),
            pl.BlockSpec((1, w), lambda b, t: (0, D_RG // w)),
            pl.BlockSpec((1, LANES), lambda b, t: (0, 0)),
            pl.BlockSpec((1, ML_HV), lambda b, t: (0, 0)),
            pl.BlockSpec((ns, HALO, w), lambda b, t: (b, 0, D_RG // w)),
            st4(ML_HEADS, ML_DK, ML_DV),
            st4(ML_HEADS, 1, ML_DK),
            st4(ML_HEADS, 1, LANES),
        ],
        out_specs=[
            pl.BlockSpec((tm, ML_HV), lambda b, t: (b * tps + t, 0)),
            pl.BlockSpec((ns, CONV_K - 1, w), lambda b, t: (b, 0, 0)),
            st4(ML_HEADS, ML_DK, ML_DV),
            st4(ML_HEADS, 1, ML_DK),
            st4(ML_HEADS, 1, LANES),
        ],
        out_shape=[
            jax.ShapeDtypeStruct((m, ML_HV), BF16),
            jax.ShapeDtypeStruct((nb, CONV_K - 1, w), F32),
            jax.ShapeDtypeStruct((nb, ML_HEADS, ML_DK, ML_DV), F32),
            jax.ShapeDtypeStruct((nb, ML_HEADS, 1, ML_DK), F32),
            jax.ShapeDtypeStruct((nb, ML_HEADS, 1, LANES), F32),
        ],
        scratch_shapes=[pltpu.VMEM((ns, HALO + rows, w), F32)],
        compiler_params=_params("parallel", "arbitrary"),
        name="mlstm",
    )(proj, proj, proj, gates, cw, cb, gbias, ng, bufp, c0, n0, m0)


def _layernorm(v, g_ref, b_ref):
    mu = jnp.mean(v, axis=-1, keepdims=True)
    vc = v - mu
    var = jnp.mean(vc * vc, axis=-1, keepdims=True)
    return vc * lax.rsqrt(var + EPS) * g_ref[...] + b_ref[...]


def _sgu_long_kernel(u_ref, v_ref, lg_ref, lb_ref, ws_ref, sb_ref, out_ref):
    vn = _layernorm(v_ref[...], lg_ref, lb_ref).astype(BF16)
    t_idx = lax.broadcasted_iota(jnp.int32, (C_CHUNK, C_CHUNK), 0)
    s_idx = lax.broadcasted_iota(jnp.int32, (C_CHUNK, C_CHUNK), 1)
    for g in range(C_GROUPS):
        sl = slice(g * C_GW, (g + 1) * C_GW)
        w = jnp.where(t_idx >= s_idx, ws_ref[g], 0.0).astype(BF16)
        mix = jnp.dot(w, vn[:, sl], preferred_element_type=F32) + sb_ref[:, g:g + 1]
        out_ref[:, sl] = (u_ref[:, sl] * mix).astype(out_ref.dtype)


def sgu_long(uv, lg, lb, ws, sb_t):
    m = uv.shape[0]
    return pl.pallas_call(
        _sgu_long_kernel,
        grid=(m // C_CHUNK,),
        in_specs=[
            pl.BlockSpec((C_CHUNK, D_C), lambda i: (i, 0)),
            pl.BlockSpec((C_CHUNK, D_C), lambda i: (i, 1)),
            pl.BlockSpec((1, D_C), lambda i: (0, 0)),
            pl.BlockSpec((1, D_C), lambda i: (0, 0)),
            pl.BlockSpec((C_GROUPS, C_CHUNK, C_CHUNK), lambda i: (0, 0, 0)),
            pl.BlockSpec((C_CHUNK, C_GROUPS), lambda i: (0, 0)),
        ],
        out_specs=pl.BlockSpec((C_CHUNK, D_C), lambda i: (i, 0)),
        out_shape=jax.ShapeDtypeStruct((m, D_C), BF16),
        compiler_params=_params("parallel"),
        name="sgu_long",
    )(uv, uv, lg, lb, ws, sb_t)


def _sgu_short_kernel(u_ref, v_ref, lg_ref, lb_ref, we_ref, be_ref, out_ref, vn_ref, *, ns, rows):
    vn = _layernorm(v_ref[...], lg_ref, lb_ref)
    vn_ref[...] = vn
    v3 = vn.reshape(ns, rows, D_C)
    mix = jnp.broadcast_to(be_ref[...], (ns, rows, D_C))
    for s in range(rows):
        mix = mix + we_ref[s] * v3[:, s:s + 1, :]
    out_ref[...] = (u_ref[...] * mix.reshape(ns * rows, D_C)).astype(out_ref.dtype)


def sgu_short(uv, lg, lb, w_exp, b_exp, rows, ns=16):
    m = uv.shape[0]
    tm = ns * rows
    kern = functools.partial(_sgu_short_kernel, ns=ns, rows=rows)
    return pl.pallas_call(
        kern,
        grid=(m // tm,),
        in_specs=[
            pl.BlockSpec((tm, D_C), lambda i: (i, 0)),
            pl.BlockSpec((tm, D_C), lambda i: (i, 1)),
            pl.BlockSpec((1, D_C), lambda i: (0, 0)),
            pl.BlockSpec((1, D_C), lambda i: (0, 0)),
            pl.BlockSpec((rows, rows, D_C), lambda i: (0, 0, 0)),
            pl.BlockSpec((rows, D_C), lambda i: (0, 0)),
        ],
        out_specs=[pl.BlockSpec((tm, D_C), lambda i: (i, 0)), pl.BlockSpec((tm, D_C), lambda i: (i, 0))],
        out_shape=[jax.ShapeDtypeStruct((m, D_C), BF16), jax.ShapeDtypeStruct((m, D_C), F32)],
        compiler_params=_params("parallel"),
        name="sgu_short",
    )(uv, uv, lg, lb, w_exp, b_exp)


def _pad_state(buf):
    return jnp.pad(buf, ((0, 0), (HALO - buf.shape[1], 0), (0, 0)))


def _block_diag(w):
    per = RG_CB // RG_BW
    w = w.reshape(D_RG // RG_CB, per, RG_BW, RG_BW)
    eye = jnp.eye(per, dtype=w.dtype)
    bd = w[:, :, :, None, :] * eye[None, :, None, :, None]
    return bd.reshape(D_RG // RG_CB, RG_CB, RG_CB).astype(BF16)


def _prep_weights(P):
    W = {}
    n_even = P["w_in_even"].shape[0]
    n_odd = P["w_in_odd"].shape[0]
    depth = P["ffn_w_up"].shape[0]
    W["in_main"] = [P["w_in_even"][j][:, :E_MAIN].astype(BF16) for j in range(n_even)]
    W["in_gate"] = [jnp.pad(P["w_in_even"][j][:, E_MAIN:], ((0, 0), (0, LANES - 2 * ML_HEADS))).astype(BF16)
                    for j in range(n_even)]
    W["wa"] = [_block_diag(P["rg_wa"][j]) for j in range(n_even)]
    W["wx"] = [_block_diag(P["rg_wx"][j]) for j in range(n_even)]
    W["gbias"] = [jnp.pad(P["ml_gate_b"][j].reshape(1, 2 * ML_HEADS), ((0, 0), (0, LANES - 2 * ML_HEADS)))
                  for j in range(n_even)]
    W["out_even"] = [P["w_out_even"][j].astype(BF16) for j in range(n_even)]
    W["in_odd"] = [P["w_in_odd"][j].astype(BF16) for j in range(n_odd)]
    W["out_odd"] = [P["w_out_odd"][j].astype(BF16) for j in range(n_odd)]
    W["up_g"] = [P["ffn_w_up"][l][:, :D_FF].astype(BF16) for l in range(depth)]
    W["up_u"] = [P["ffn_w_up"][l][:, D_FF:].astype(BF16) for l in range(depth)]
    W["down"] = [P["ffn_w_down"][l].astype(BF16) for l in range(depth)]
    return W


def _sgu_short_weights(ws, sb, rows):
    w = jnp.tril(ws)[:, :rows, :rows]
    w_exp = jnp.repeat(jnp.transpose(w, (2, 1, 0)), C_GW, axis=2)
    b_exp = jnp.repeat(jnp.transpose(sb[:, :rows]), C_GW, axis=1)
    return w_exp, b_exp


def _trunk(x, conv_buf, rg_h, ml_c, ml_n, ml_m, ffn_buf, P, W, short):
    nb, seq, d = x.shape
    m = nb * seq
    depth = P["ffn_w_up"].shape[0]
    x = x.reshape(m, d)
    if short:
        rg_rows, rg_ns = seq, 32
        ml_rows, ml_ns = seq, 8
    else:
        rg_rows, rg_ns = 256, 1
        ml_rows, ml_ns = 128, 1
    conv_l, h_l, c_l, n_l, m_l, v_l, f_l = [], [], [], [], [], [], []
    for layer in range(depth):
        j = layer // 2
        xn = rmsnorm(x, P["norm_mix"][layer], BF16)
        if layer % 2 == 0:
            proj = matmul([xn], W["in_main"][j])
            gates = matmul([xn], W["in_gate"][j])
            bufp = _pad_state(conv_buf[j])
            cw, cb = P["conv_even_w"][j], P["conv_even_b"][j].reshape(1, CONV_W)
            rg_out, cst_rg, h_last = rglru(
                proj, cw, cb, W["wa"][j], W["wx"][j], P["rg_ba"][j].reshape(1, D_RG), P["rg_bx"][j].reshape(1, D_RG),
                P["rg_lambda"][j].reshape(1, D_RG), bufp, rg_h[j].reshape(nb, 1, D_RG), seq, rg_rows, rg_ns)
            ml_out, cst_ml, c_new, n_new, m_new = mlstm(
                proj, gates, cw, cb, W["gbias"][j], P["ml_norm_g"][j].reshape(1, ML_HV), bufp, ml_c[j],
                ml_n[j].reshape(nb, ML_HEADS, 1, ML_DK),
                jnp.broadcast_to(ml_m[j][:, :, None, None], (nb, ML_HEADS, 1, LANES)), seq, ml_rows, ml_ns)
            x = matmul([rg_out, ml_out], W["out_even"][j], res=x)
            conv_l.append(jnp.concatenate([cst_rg, cst_ml], axis=-1))
            h_l.append(h_last.reshape(nb, D_RG))
            c_l.append(c_new)
            n_l.append(n_new.reshape(nb, ML_HEADS, ML_DK))
            m_l.append(m_new[:, :, 0, 0])
        else:
            uv = matmul([xn], W["in_odd"][j], act="gelu")
            lg, lb = P["sgu_ln_g"][j].reshape(1, D_C), P["sgu_ln_b"][j].reshape(1, D_C)
            if short:
                w_exp, b_exp = _sgu_short_weights(P["sgu_ws"][j], P["sgu_b"][j], seq)
                gated, vn = sgu_short(uv, lg, lb, w_exp, b_exp, seq)
                v_l.append(vn.reshape(nb, seq, D_C))
            else:
                gated = sgu_long(uv, lg, lb, P["sgu_ws"][j], jnp.transpose(P["sgu_b"][j]))
            x = matmul([gated], W["out_odd"][j], res=x)
        xn = rmsnorm(x, P["norm_ffn"][layer], BF16)
        hid, f_new = ffn_up(xn, W["up_g"][layer], W["up_u"][layer], P["ffn_conv_w"][layer],
                            P["ffn_conv_b"][layer].reshape(1, D_FF), _pad_state(ffn_buf[layer]), seq)
        f_l.append(f_new)
        x = matmul([hid], W["down"][layer], res=x, tk=512)
    y = rmsnorm(x, P["norm_final"], F32).reshape(nb, seq, d)
    return (y, jnp.stack(conv_l), jnp.stack(h_l), jnp.stack(c_l), jnp.stack(n_l), jnp.stack(m_l), v_l,
            jnp.stack(f_l))


def kernel(x_prompt, x_sample, state_conv_mix, state_rglru_h, state_mlstm_C, state_mlstm_n, state_mlstm_m, state_ffn_conv, norm_mix, norm_ffn, norm_final, w_in_even, conv_even_w, conv_even_b, rg_wa, rg_ba, rg_wx, rg_bx, rg_lambda, ml_gate_b, ml_norm_g, w_out_even, w_in_odd, sgu_ln_g, sgu_ln_b, sgu_ws, sgu_b, w_out_odd, ffn_w_up, ffn_conv_w, ffn_conv_b, ffn_w_down):
    P = {
        "norm_mix": norm_mix, "norm_ffn": norm_ffn, "norm_final": norm_final,
        "w_in_even": w_in_even, "conv_even_w": conv_even_w, "conv_even_b": conv_even_b,
        "rg_wa": rg_wa, "rg_ba": rg_ba, "rg_wx": rg_wx, "rg_bx": rg_bx, "rg_lambda": rg_lambda,
        "ml_gate_b": ml_gate_b, "ml_norm_g": ml_norm_g, "w_out_even": w_out_even,
        "w_in_odd": w_in_odd, "sgu_ln_g": sgu_ln_g, "sgu_ln_b": sgu_ln_b, "sgu_ws": sgu_ws,
        "sgu_b": sgu_b, "w_out_odd": w_out_odd, "ffn_w_up": ffn_w_up, "ffn_conv_w": ffn_conv_w,
        "ffn_conv_b": ffn_conv_b, "ffn_w_down": ffn_w_down,
    }
    W = _prep_weights(P)
    n_even, depth = w_in_even.shape[0], ffn_w_up.shape[0]
    bp = x_prompt.shape[0]
    dt = x_prompt.dtype
    z_conv = jnp.zeros((n_even, bp, CONV_K - 1, CONV_W), dt)
    z_h = jnp.zeros((n_even, bp, D_RG), dt)
    z_c = jnp.zeros((n_even, bp, ML_HEADS, ML_DK, ML_DV), dt)
    z_n = jnp.zeros((n_even, bp, ML_HEADS, ML_DK), dt)
    z_m = jnp.zeros((n_even, bp, ML_HEADS), dt)
    z_f = jnp.zeros((depth, bp, FFN_K - 1, D_FF), dt)
    y_p, conv_p, h_p, c_p, n_p, m_p, _, f_p = _trunk(x_prompt, z_conv, z_h, z_c, z_n, z_m, z_f, P, W, short=False)
    y_s, conv_s, h_s, c_s, n_s, m_s, v_list, f_s = _trunk(
        x_sample, state_conv_mix, state_rglru_h, state_mlstm_C, state_mlstm_n, state_mlstm_m, state_ffn_conv,
        P, W, short=True)
    chunk_v_s = jnp.stack(v_list)
    return (y_p, y_s, conv_p, conv_s, h_p, h_s, c_p, c_s, n_p, n_s, m_p, m_s, chunk_v_s, f_p, f_s)
```

```python
import functools

import jax
import jax.numpy as jnp
from jax import lax
from jax.experimental import pallas as pl
from jax.experimental.pallas import tpu as pltpu

F32 = jnp.float32
BF16 = jnp.bfloat16

D_MODEL = 2048
EPS = 1e-6
D_RG = D_MODEL // 2
RG_BLOCKS = 16
RG_BW = D_RG // RG_BLOCKS
RG_C = 8.0
CONV_K = 4
ML_HEADS = 4
ML_DK = D_MODEL // 16
ML_DV = D_MODEL // 8
ML_HK = ML_HEADS * ML_DK
ML_HV = ML_HEADS * ML_DV
CONV_W = D_RG + 2 * ML_HK
E_MAIN = CONV_W + D_RG + 2 * ML_HV
D_C = D_MODEL
C_GROUPS = 8
C_GW = D_C // C_GROUPS
C_CHUNK = 128
D_FF = 5632
FFN_K = 3

SUBLANES = 8
LANES = 128
HALO = SUBLANES
RG_CB = 256
VMEM_LIMIT = 56 * 1024 * 1024
ROW_CHUNK = 256


def _params(*sem):
    return pltpu.CompilerParams(dimension_semantics=sem, vmem_limit_bytes=VMEM_LIMIT)


def _rmsnorm_kernel(x_ref, g_ref, o_ref):
    x = x_ref[...]
    ms = jnp.mean(x * x, axis=-1, keepdims=True)
    o_ref[...] = (x * lax.rsqrt(ms + EPS) * g_ref[...]).astype(o_ref.dtype)


def rmsnorm(x, g, out_dtype, tm=512):
    m, d = x.shape
    return pl.pallas_call(
        _rmsnorm_kernel,
        grid=(m // tm,),
        in_specs=[pl.BlockSpec((tm, d), lambda i: (i, 0)), pl.BlockSpec((1, d), lambda i: (0, 0))],
        out_specs=pl.BlockSpec((tm, d), lambda i: (i, 0)),
        out_shape=jax.ShapeDtypeStruct((m, d), out_dtype),
        compiler_params=_params("parallel"),
        name="rmsnorm",
    )(x, g.reshape(1, d))


def _row_chunk(tm):
    return min(tm, ROW_CHUNK)


def _load_weight(w_ref, ws_ref, first):
    @pl.when(first)
    def _():
        ws_ref[...] = w_ref[0].astype(BF16)


def _mm_nt_kernel(a_ref, w_ref, o_ref, ws_ref, *, act):
    _load_weight(w_ref, ws_ref, pl.program_id(1) == 0)
    tm = a_ref.shape[0]
    rc = _row_chunk(tm)
    for c in range(tm // rc):
        rs = slice(c * rc, (c + 1) * rc)
        acc = jnp.dot(a_ref[rs, :], ws_ref[...], preferred_element_type=F32)
        if act == "gelu":
            acc = jax.nn.gelu(acc)
        o_ref[rs, :] = acc.astype(o_ref.dtype)


def matmul_nt(a, w, layer, n_out, act=None, tm=1024, tn=1024):
    m, k = a.shape
    tm, tn = min(tm, m), min(tn, n_out)
    return pl.pallas_call(
        functools.partial(_mm_nt_kernel, act=act),
        grid=(n_out // tn, m // tm),
        in_specs=[
            pl.BlockSpec((tm, k), lambda j, i: (i, 0)),
            pl.BlockSpec((1, k, tn), lambda j, i: (layer, 0, j)),
        ],
        out_specs=pl.BlockSpec((tm, tn), lambda j, i: (i, j)),
        out_shape=jax.ShapeDtypeStruct((m, n_out), F32),
        scratch_shapes=[pltpu.VMEM((k, tn), BF16)],
        compiler_params=_params("parallel", "arbitrary"),
        name="matmul_nt",
    )(a, w)


def _mm_full_kernel(*refs, ka, cast_w, norm, emit_x):
    n_a = len(ka)
    a_refs, w_ref, res_ref = refs[:n_a], refs[n_a], refs[n_a + 1]
    pos = n_a + 2
    g_ref = refs[pos] if norm else None
    pos += int(norm)
    x_ref = refs[pos] if emit_x else None
    pos += int(emit_x)
    xn_ref = refs[pos] if norm else None
    if cast_w:
        ws_ref = refs[-1]
        _load_weight(w_ref, ws_ref, pl.program_id(0) == 0)
        wget = lambda lo, hi: ws_ref[lo:hi, :]
    else:
        wget = lambda lo, hi: w_ref[0, lo:hi, :]
    tm = res_ref.shape[0]
    rc = _row_chunk(tm)
    for c in range(tm // rc):
        rs = slice(c * rc, (c + 1) * rc)
        y = res_ref[rs, :]
        off = 0
        for a_ref, k in zip(a_refs, ka):
            y = y + jnp.dot(a_ref[rs, :], wget(off, off + k), preferred_element_type=F32)
            off += k
        if emit_x:
            x_ref[rs, :] = y
        if norm:
            ms = jnp.mean(y * y, axis=-1, keepdims=True)
            xn_ref[rs, :] = (y * lax.rsqrt(ms + EPS) * g_ref[...]).astype(xn_ref.dtype)


def matmul_res_norm(a_list, w, layer, res, g=None, xn_dtype=BF16, emit_x=True, tm=512):
    m, n = res.shape
    ka = tuple(a.shape[1] for a in a_list)
    kt = sum(ka)
    cast_w = w.dtype != BF16
    norm = g is not None
    row = lambda width: pl.BlockSpec((tm, width), lambda i: (i, 0))
    in_specs = [row(k) for k in ka]
    in_specs.append(pl.BlockSpec((1, kt, n), lambda i: (layer, 0, 0), pipeline_mode=pl.Buffered(1)))
    in_specs.append(row(n))
    args = list(a_list) + [w, res]
    if norm:
        in_specs.append(pl.BlockSpec((1, n), lambda i: (0, 0)))
        args.append(g.reshape(1, n))
    out_specs, out_shape = [], []
    if emit_x:
        out_specs.append(row(n))
        out_shape.append(jax.ShapeDtypeStruct((m, n), F32))
    if norm:
        out_specs.append(row(n))
        out_shape.append(jax.ShapeDtypeStruct((m, n), xn_dtype))
    outs = pl.pallas_call(
        functools.partial(_mm_full_kernel, ka=ka, cast_w=cast_w, norm=norm, emit_x=emit_x),
        grid=(m // tm,),
        in_specs=in_specs,
        out_specs=out_specs,
        out_shape=out_shape,
        scratch_shapes=[pltpu.VMEM((kt, n), BF16)] if cast_w else [],
        compiler_params=_params("arbitrary"),
        name="matmul_res_norm",
    )(*args)
    x_new = outs[0] if emit_x else None
    xn = outs[-1] if norm else None
    return x_new, xn


def _halo_init(xs_ref, buf_ref, first):
    @pl.when(first)
    def _():
        xs_ref[:, 0:HALO, :] = buf_ref[...]


def _conv_chunk(xs_ref, x3, w_ref, b_ref, taps, seqs, r0):
    n = x3.shape[1]
    xs_ref[seqs, HALO + r0:HALO + r0 + n, :] = x3
    y = b_ref[...] + w_ref[taps - 1:taps, :] * x3
    for k in range(taps - 1):
        lo = HALO + r0 - (taps - 1) + k
        y = y + w_ref[k:k + 1, :] * xs_ref[seqs, lo:lo + n, :]
    return y


def _halo_carry(xs_ref, rows):
    xs_ref[:, 0:HALO, :] = xs_ref[:, rows:rows + HALO, :]


def _causal_conv(xs_ref, x3, buf_ref, w_ref, b_ref, first, rows, taps, carry):
    _halo_init(xs_ref, buf_ref, first)
    y = _conv_chunk(xs_ref, x3, w_ref, b_ref, taps, slice(None), 0)
    if carry:
        _halo_carry(xs_ref, rows)
    return y


def _ffn_up_kernel(a_ref, wg_ref, wu_ref, cw_ref, cb_ref, buf_ref, hid_ref, st_ref, wgs_ref, wus_ref, gs_ref,
                   *, ns, rows, tps):
    i = pl.program_id(1)
    _load_weight(wg_ref, wgs_ref, i == 0)
    _load_weight(wu_ref, wus_ref, i == 0)
    _halo_init(gs_ref, buf_ref, (i % tps) == 0)
    tn = wgs_ref.shape[1]
    tm = ns * rows
    rc = _row_chunk(tm)
    for c in range(tm // rc):
        a = a_ref[c * rc:(c + 1) * rc, :]
        if ns == 1:
            seqs, r0, shape = slice(0, 1), c * rc, (1, rc, tn)
        else:
            nsc = rc // rows
            seqs, r0, shape = slice(c * nsc, (c + 1) * nsc), 0, (nsc, rows, tn)
        g3 = jnp.dot(a, wgs_ref[...], preferred_element_type=F32).reshape(shape)
        u3 = jnp.dot(a, wus_ref[...], preferred_element_type=F32).reshape(shape)
        gc = _conv_chunk(gs_ref, g3, cw_ref, cb_ref, FFN_K, seqs, r0)
        hid_ref[c * rc:(c + 1) * rc, :] = (jax.nn.gelu(gc) * u3).reshape(rc, tn).astype(hid_ref.dtype)
    st_ref[...] = gs_ref[:, HALO + rows - (FFN_K - 1):HALO + rows, :]
    if tps > 1:
        _halo_carry(gs_ref, rows)


def ffn_up(xn, w_up, layer, cw, cb, bufp, seq, tm=1024, tn=512):
    m, d = xn.shape
    nb = m // seq
    tm = min(tm, m)
    rows = min(seq, tm)
    ns = tm // rows
    tps = seq // rows
    nj = D_FF // tn
    kern = functools.partial(_ffn_up_kernel, ns=ns, rows=rows, tps=tps)
    return pl.pallas_call(
        kern,
        grid=(nj, m // tm),
        in_specs=[
            pl.BlockSpec((tm, d), lambda j, i: (i, 0)),
            pl.BlockSpec((1, d, tn), lambda j, i: (layer, 0, j)),
            pl.BlockSpec((1, d, tn), lambda j, i: (layer, 0, nj + j)),
            pl.BlockSpec((FFN_K, tn), lambda j, i: (0, j)),
            pl.BlockSpec((1, tn), lambda j, i: (0, j)),
            pl.BlockSpec((ns, HALO, tn), lambda j, i: (i // tps, 0, j)),
        ],
        out_specs=[
            pl.BlockSpec((tm, tn), lambda j, i: (i, j)),
            pl.BlockSpec((ns, FFN_K - 1, tn), lambda j, i: (i // tps, 0, j)),
        ],
        out_shape=[
            jax.ShapeDtypeStruct((m, D_FF), BF16),
            jax.ShapeDtypeStruct((nb, FFN_K - 1, D_FF), F32),
        ],
        scratch_shapes=[pltpu.VMEM((d, tn), BF16), pltpu.VMEM((d, tn), BF16),
                        pltpu.VMEM((ns, HALO + rows, tn), F32)],
        compiler_params=_params("parallel", "arbitrary"),
        name="ffn_up",
    )(xn, w_up, w_up, cw, cb, bufp)


def _seq_scan(a, b, rows):
    pos = lax.broadcasted_iota(jnp.int32, a.shape, 0) & (rows - 1)
    d = 1
    while d < rows:
        m = pos >= d
        a_sh = jnp.where(m, pltpu.roll(a, d, 0), 1.0)
        b_sh = jnp.where(m, pltpu.roll(b, d, 0), 0.0)
        b = a * b_sh + b
        a = a * a_sh
        d *= 2
    return a, b


def _rglru_kernel(x_ref, gate_ref, cw_ref, cb_ref, wa_ref, wx_ref, ba_ref, bx_ref, lam_ref, buf_ref, h0_ref,
                  out_ref, cst_ref, hl_ref, xs_ref, *, ns, rows, tps):
    c = x_ref.shape[1]
    first = pl.program_id(2) == 0

    @pl.when(first)
    def _():
        hl_ref[...] = h0_ref[...]

    x3 = x_ref[...].reshape(ns, rows, c)
    xc = _causal_conv(xs_ref, x3, buf_ref, cw_ref, cb_ref, first, rows, CONV_K, tps > 1)
    cst_ref[...] = x3[:, rows - (CONV_K - 1):rows, :]
    xc = xc.reshape(ns * rows, c)
    xb = xc.astype(BF16)
    r = jax.nn.sigmoid(jnp.dot(xb, wa_ref[0], preferred_element_type=F32) + ba_ref[...])
    i = jax.nn.sigmoid(jnp.dot(xb, wx_ref[0], preferred_element_type=F32) + bx_ref[...])
    nl = -lam_ref[...]
    softplus = jnp.maximum(nl, 0.0) + jnp.log1p(jnp.exp(-jnp.abs(nl)))
    log_a = (-RG_C) * r * softplus
    th = jnp.tanh(log_a)
    u = jnp.sqrt(-2.0 * th / (1.0 - th)) * (i * xc)
    a_cum, b_cum = _seq_scan(jnp.exp(log_a), u, rows)
    h = a_cum.reshape(ns, rows, c) * hl_ref[...] + b_cum.reshape(ns, rows, c)
    hl_ref[...] = h[:, rows - 1:rows, :]
    out_ref[...] = (jax.nn.gelu(gate_ref[...]) * h.reshape(ns * rows, c)).astype(out_ref.dtype)


def rglru(proj, cw, cb, wa_bd, wx_bd, ba, bx, lam, bufp, h0, seq, rows, ns):
    m = proj.shape[0]
    nb = m // seq
    tps = seq // rows
    tm = ns * rows
    c = RG_CB
    gate0 = CONV_W // c
    kern = functools.partial(_rglru_kernel, ns=ns, rows=rows, tps=tps)
    vec = pl.BlockSpec((1, c), lambda b, cb_, t: (0, cb_))
    return pl.pallas_call(
        kern,
        grid=(nb // ns, D_RG // c, tps),
        in_specs=[
            pl.BlockSpec((tm, c), lambda b, cb_, t: (b * tps + t, cb_)),
            pl.BlockSpec((tm, c), lambda b, cb_, t: (b * tps + t, gate0 + cb_)),
            pl.BlockSpec((CONV_K, c), lambda b, cb_, t: (0, cb_)),
            vec,
            pl.BlockSpec((1, c, c), lambda b, cb_, t: (cb_, 0, 0)),
            pl.BlockSpec((1, c, c), lambda b, cb_, t: (cb_, 0, 0)),
            vec, vec, vec,
            pl.BlockSpec((ns, HALO, c), lambda b, cb_, t: (b, 0, cb_)),
            pl.BlockSpec((ns, 1, c), lambda b, cb_, t: (b, 0, cb_)),
        ],
        out_specs=[
            pl.BlockSpec((tm, c), lambda b, cb_, t: (b * tps + t, cb_)),
            pl.BlockSpec((ns, CONV_K - 1, c), lambda b, cb_, t: (b, 0, cb_)),
            pl.BlockSpec((ns, 1, c), lambda b, cb_, t: (b, 0, cb_)),
        ],
        out_shape=[
            jax.ShapeDtypeStruct((m, D_RG), BF16),
            jax.ShapeDtypeStruct((nb, CONV_K - 1, D_RG), F32),
            jax.ShapeDtypeStruct((nb, 1, D_RG), F32),
        ],
        scratch_shapes=[pltpu.VMEM((ns, HALO + rows, c), F32)],
        compiler_params=_params("parallel", "parallel", "arbitrary"),
        name="rglru",
    )(proj, proj, cw, cb, wa_bd, wx_bd, ba, bx, lam, bufp, h0)


def _heads(x3, width, ns, rows):
    parts = [x3[:, None, :, h * width:(h + 1) * width] for h in range(ML_HEADS)]
    return jnp.concatenate(parts, axis=1).reshape(ns * ML_HEADS, rows, width)


def _mlstm_kernel(qk_ref, v_ref, o_ref, g_ref, cw_ref, cb_ref, gb_ref, ng_ref, buf_ref, c0_ref, n0_ref, m0_ref,
                  out_ref, cst_ref, c_ref, n_ref, m_ref, xs_ref, *, ns, rows, tps):
    G = ns * ML_HEADS
    first = pl.program_id(1) == 0

    @pl.when(first)
    def _():
        c_ref[...] = c0_ref[...]
        n_ref[...] = n0_ref[...]
        m_ref[...] = m0_ref[...]

    x3 = qk_ref[...].reshape(ns, rows, 2 * ML_HK)
    xc = _causal_conv(xs_ref, x3, buf_ref, cw_ref, cb_ref, first, rows, CONV_K, tps > 1)
    cst_ref[...] = x3[:, rows - (CONV_K - 1):rows, :]
    qk = jax.nn.silu(xc)
    q = _heads(qk[:, :, :ML_HK], ML_DK, ns, rows)
    k = _heads(qk[:, :, ML_HK:], ML_DK, ns, rows) * (ML_DK ** -0.5)
    v = _heads(v_ref[...].reshape(ns, rows, ML_HV), ML_DV, ns, rows)
    gates = g_ref[...].reshape(ns, rows, LANES) + gb_ref[...]
    i_col = _heads(gates[:, :, :ML_HEADS], 1, ns, rows)
    f_col = _heads(jax.nn.log_sigmoid(gates[:, :, ML_HEADS:2 * ML_HEADS]), 1, ns, rows)

    t_idx = lax.broadcasted_iota(jnp.int32, (rows, rows), 0)
    s_idx = lax.broadcasted_iota(jnp.int32, (rows, rows), 1)
    eye = t_idx == s_idx
    causal = t_idx >= s_idx
    i_row = jnp.sum(jnp.where(eye, i_col, 0.0), axis=1, keepdims=True)
    f_row = jnp.sum(jnp.where(eye, f_col, 0.0), axis=1, keepdims=True)
    b_col = jnp.sum(jnp.where(causal, f_row, 0.0), axis=2, keepdims=True)
    b_row = jnp.sum(jnp.where(t_idx <= s_idx, f_col, 0.0), axis=1, keepdims=True)
    dmat = jnp.where(causal, b_col - b_row + i_row, -jnp.inf)

    m_prev = m_ref[...].reshape(G, 1, LANES)[:, :, 0:1]
    inter = b_col + m_prev
    m_t = jnp.maximum(inter, jnp.max(dmat, axis=2, keepdims=True))
    qb, kb, vb = q.astype(BF16), k.astype(BF16), v.astype(BF16)
    s = jnp.einsum("gld,gsd->gls", qb, kb, preferred_element_type=F32) * jnp.exp(dmat - m_t)
    w_inter = jnp.exp(inter - m_t)
    c_old = c_ref[...].reshape(G, ML_DK, ML_DV)
    n_old = n_ref[...].reshape(G, 1, ML_DK)
    num = (w_inter * jnp.einsum("gld,gde->gle", qb, c_old.astype(BF16), preferred_element_type=F32)
           + jnp.einsum("gls,gse->gle", s.astype(BF16), vb, preferred_element_type=F32))
    den = w_inter * jnp.sum(q * n_old, axis=2, keepdims=True) + jnp.sum(s, axis=2, keepdims=True)
    h = num / jnp.maximum(jnp.abs(den), jnp.exp(-m_t))

    m_new = m_t[:, rows - 1:rows, :]
    b_last = b_col[:, rows - 1:rows, :]
    w_state = jnp.exp(b_last - b_col + i_col - m_new)
    decay = jnp.exp(b_last + m_prev - m_new)
    kw = k * w_state
    kwt = jnp.swapaxes(kw, 1, 2).astype(BF16)
    c_new = decay * c_old + jnp.einsum("gds,gse->gde", kwt, vb, preferred_element_type=F32)
    n_new = decay * n_old + jnp.sum(kw, axis=1, keepdims=True)
    c_ref[...] = c_new.reshape(ns, ML_HEADS, ML_DK, ML_DV)
    n_ref[...] = n_new.reshape(ns, ML_HEADS, 1, ML_DK)
    m_ref[...] = jnp.broadcast_to(m_new, (G, 1, LANES)).reshape(ns, ML_HEADS, 1, LANES)

    hn = h * lax.rsqrt(jnp.mean(h * h, axis=2, keepdims=True) + EPS)
    hn = hn.reshape(ns, ML_HEADS, rows, ML_DV)
    o3 = o_ref[...].reshape(ns, rows, ML_HV)
    for hh in range(ML_HEADS):
        sl = slice(hh * ML_DV, (hh + 1) * ML_DV)
        y = hn[:, hh] * ng_ref[:, sl] * jax.nn.sigmoid(o3[:, :, sl])
        out_ref[:, sl] = y.reshape(ns * rows, ML_DV).astype(out_ref.dtype)


def mlstm(proj, gates, cw, cb, gbias, ng, bufp, c0, n0, m0, seq, rows, ns):
    m = proj.shape[0]
    nb = m // seq
    tps = seq // rows
    tm = ns * rows
    w = 2 * ML_HK
    kern = functools.partial(_mlstm_kernel, ns=ns, rows=rows, tps=tps)
    row_blk = lambda col: pl.BlockSpec((tm, w), lambda b, t: (b * tps + t, col))
    st4 = lambda *shape: pl.BlockSpec((ns,) + shape, lambda b, t: (b, 0, 0, 0))
    return pl.pallas_call(
        kern,
        grid=(nb // ns, tps),
        in_specs=[
            row_blk(D_RG // w),
            row_blk((CONV_W + D_RG) // w),
            row_blk((CONV_W + D_RG + ML_HV) // w),
            pl.BlockSpec((tm, LANES), lambda b, t: (b * tps + t, 0)),
            pl.BlockSpec((CONV_K, w), lambda b, t: (0, D_RG // w)),
            pl.BlockSpec((1, w), lambda b, t: (0, D_RG // w)),
            pl.BlockSpec((1, LANES), lambda b, t: (0, 0)),
            pl.BlockSpec((1, ML_HV), lambda b, t: (0, 0)),
            pl.BlockSpec((ns, HALO, w), lambda b, t: (b, 0, D_RG // w)),
            st4(ML_HEADS, ML_DK, ML_DV),
            st4(ML_HEADS, 1, ML_DK),
            st4(ML_HEADS, 1, LANES),
        ],
        out_specs=[
            pl.BlockSpec((tm, ML_HV), lambda b, t: (b * tps + t, 0)),
            pl.BlockSpec((ns, CONV_K - 1, w), lambda b, t: (b, 0, 0)),
            st4(ML_HEADS, ML_DK, ML_DV),
            st4(ML_HEADS, 1, ML_DK),
            st4(ML_HEADS, 1, LANES),
        ],
        out_shape=[
            jax.ShapeDtypeStruct((m, ML_HV), BF16),
            jax.ShapeDtypeStruct((nb, CONV_K - 1, w), F32),
            jax.ShapeDtypeStruct((nb, ML_HEADS, ML_DK, ML_DV), F32),
            jax.ShapeDtypeStruct((nb, ML_HEADS, 1, ML_DK), F32),
            jax.ShapeDtypeStruct((nb, ML_HEADS, 1, LANES), F32),
        ],
        scratch_shapes=[pltpu.VMEM((ns, HALO + rows, w), F32)],
        compiler_params=_params("parallel", "arbitrary"),
        name="mlstm",
    )(proj, proj, proj, gates, cw, cb, gbias, ng, bufp, c0, n0, m0)


def _layernorm(v, g_ref, b_ref):
    mu = jnp.mean(v, axis=-1, keepdims=True)
    vc = v - mu
    var = jnp.mean(vc * vc, axis=-1, keepdims=True)
    return vc * lax.rsqrt(var + EPS) * g_ref[...] + b_ref[...]


def _sgu_long_kernel(u_ref, v_ref, lg_ref, lb_ref, ws_ref, sb_ref, out_ref):
    vn = _layernorm(v_ref[...], lg_ref, lb_ref).astype(BF16)
    t_idx = lax.broadcasted_iota(jnp.int32, (C_CHUNK, C_CHUNK), 0)
    s_idx = lax.broadcasted_iota(jnp.int32, (C_CHUNK, C_CHUNK), 1)
    for g in range(C_GROUPS):
        sl = slice(g * C_GW, (g + 1) * C_GW)
        w = jnp.where(t_idx >= s_idx, ws_ref[g], 0.0).astype(BF16)
        mix = jnp.dot(w, vn[:, sl], preferred_element_type=F32) + sb_ref[:, g:g + 1]
        out_ref[:, sl] = (u_ref[:, sl] * mix).astype(out_ref.dtype)


def sgu_long(uv, lg, lb, ws, sb_t):
    m = uv.shape[0]
    return pl.pallas_call(
        _sgu_long_kernel,
        grid=(m // C_CHUNK,),
        in_specs=[
            pl.BlockSpec((C_CHUNK, D_C), lambda i: (i, 0)),
            pl.BlockSpec((C_CHUNK, D_C), lambda i: (i, 1)),
            pl.BlockSpec((1, D_C), lambda i: (0, 0)),
            pl.BlockSpec((1, D_C), lambda i: (0, 0)),
            pl.BlockSpec((C_GROUPS, C_CHUNK, C_CHUNK), lambda i: (0, 0, 0)),
            pl.BlockSpec((C_CHUNK, C_GROUPS), lambda i: (0, 0)),
        ],
        out_specs=pl.BlockSpec((C_CHUNK, D_C), lambda i: (i, 0)),
        out_shape=jax.ShapeDtypeStruct((m, D_C), BF16),
        compiler_params=_params("parallel"),
        name="sgu_long",
    )(uv, uv, lg, lb, ws, sb_t)


def _sgu_short_kernel(u_ref, v_ref, lg_ref, lb_ref, we_ref, be_ref, out_ref, vn_ref, *, ns, rows):
    vn = _layernorm(v_ref[...], lg_ref, lb_ref)
    vn_ref[...] = vn
    v3 = vn.reshape(ns, rows, D_C)
    mix = jnp.broadcast_to(be_ref[...], (ns, rows, D_C))
    for s in range(rows):
        mix = mix + we_ref[s] * v3[:, s:s + 1, :]
    out_ref[...] = (u_ref[...] * mix.reshape(ns * rows, D_C)).astype(out_ref.dtype)


def sgu_short(uv, lg, lb, w_exp, b_exp, rows, ns=16):
    m = uv.shape[0]
    tm = ns * rows
    kern = functools.partial(_sgu_short_kernel, ns=ns, rows=rows)
    return pl.pallas_call(
        kern,
        grid=(m // tm,),
        in_specs=[
            pl.BlockSpec((tm, D_C), lambda i: (i, 0)),
            pl.BlockSpec((tm, D_C), lambda i: (i, 1)),
            pl.BlockSpec((1, D_C), lambda i: (0, 0)),
            pl.BlockSpec((1, D_C), lambda i: (0, 0)),
            pl.BlockSpec((rows, rows, D_C), lambda i: (0, 0, 0)),
            pl.BlockSpec((rows, D_C), lambda i: (0, 0)),
        ],
        out_specs=[pl.BlockSpec((tm, D_C), lambda i: (i, 0)), pl.BlockSpec((tm, D_C), lambda i: (i, 0))],
        out_shape=[jax.ShapeDtypeStruct((m, D_C), BF16), jax.ShapeDtypeStruct((m, D_C), F32)],
        compiler_params=_params("parallel"),
        name="sgu_short",
    )(uv, uv, lg, lb, w_exp, b_exp)


def _pad_state(buf):
    return jnp.pad(buf, ((0, 0), (HALO - buf.shape[1], 0), (0, 0)))


def _block_diag(w):
    per = RG_CB // RG_BW
    w = w.reshape(D_RG // RG_CB, per, RG_BW, RG_BW)
    eye = jnp.eye(per, dtype=w.dtype)
    bd = w[:, :, :, None, :] * eye[None, :, None, :, None]
    return bd.reshape(D_RG // RG_CB, RG_CB, RG_CB).astype(BF16)


def _prep_weights(P):
    n_even = P["w_in_even"].shape[0]
    pad = LANES - 2 * ML_HEADS
    return {
        "in_gate": jnp.pad(P["w_in_even"][:, :, E_MAIN:], ((0, 0), (0, 0), (0, pad))),
        "wa": [_block_diag(P["rg_wa"][j]) for j in range(n_even)],
        "wx": [_block_diag(P["rg_wx"][j]) for j in range(n_even)],
        "gbias": [jnp.pad(P["ml_gate_b"][j].reshape(1, 2 * ML_HEADS), ((0, 0), (0, pad))) for j in range(n_even)],
        "down": P["ffn_w_down"].astype(BF16),
    }


def _sgu_short_weights(ws, sb, rows):
    w = jnp.tril(ws)[:, :rows, :rows]
    w_exp = jnp.repeat(jnp.transpose(w, (2, 1, 0)), C_GW, axis=2)
    b_exp = jnp.repeat(jnp.transpose(sb[:, :rows]), C_GW, axis=1)
    return w_exp, b_exp


def _trunk(x, conv_buf, rg_h, ml_c, ml_n, ml_m, ffn_buf, P, W, short):
    nb, seq, d = x.shape
    m = nb * seq
    depth = P["ffn_w_up"].shape[0]
    x = x.reshape(m, d)
    if short:
        rg_rows, rg_ns = seq, 32
        ml_rows, ml_ns = seq, 8
    else:
        rg_rows, rg_ns = 256, 1
        ml_rows, ml_ns = 128, 1
    conv_l, h_l, c_l, n_l, m_l, v_l, f_l = [], [], [], [], [], [], []
    xn = rmsnorm(x, P["norm_mix"][0], BF16)
    for layer in range(depth):
        j = layer // 2
        g_ffn = P["norm_ffn"][layer]
        if layer % 2 == 0:
            proj = matmul_nt(xn, P["w_in_even"], j, E_MAIN)
            gates = matmul_nt(xn, W["in_gate"], j, LANES)
            bufp = _pad_state(conv_buf[j])
            cw, cb = P["conv_even_w"][j], P["conv_even_b"][j].reshape(1, CONV_W)
            rg_out, cst_rg, h_last = rglru(
                proj, cw, cb, W["wa"][j], W["wx"][j], P["rg_ba"][j].reshape(1, D_RG), P["rg_bx"][j].reshape(1, D_RG),
                P["rg_lambda"][j].reshape(1, D_RG), bufp, rg_h[j].reshape(nb, 1, D_RG), seq, rg_rows, rg_ns)
            ml_out, cst_ml, c_new, n_new, m_new = mlstm(
                proj, gates, cw, cb, W["gbias"][j], P["ml_norm_g"][j].reshape(1, ML_HV), bufp, ml_c[j],
                ml_n[j].reshape(nb, ML_HEADS, 1, ML_DK),
                jnp.broadcast_to(ml_m[j][:, :, None, None], (nb, ML_HEADS, 1, LANES)), seq, ml_rows, ml_ns)
            x, xn = matmul_res_norm([rg_out, ml_out], P["w_out_even"], j, x, g=g_ffn)
            conv_l.append(jnp.concatenate([cst_rg, cst_ml], axis=-1))
            h_l.append(h_last.reshape(nb, D_RG))
            c_l.append(c_new)
            n_l.append(n_new.reshape(nb, ML_HEADS, ML_DK))
            m_l.append(m_new[:, :, 0, 0])
        else:
            uv = matmul_nt(xn, P["w_in_odd"], j, 2 * D_C, act="gelu")
            lg, lb = P["sgu_ln_g"][j].reshape(1, D_C), P["sgu_ln_b"][j].reshape(1, D_C)
            if short:
                w_exp, b_exp = _sgu_short_weights(P["sgu_ws"][j], P["sgu_b"][j], seq)
                gated, vn = sgu_short(uv, lg, lb, w_exp, b_exp, seq)
                v_l.append(vn.reshape(nb, seq, D_C))
            else:
                gated = sgu_long(uv, lg, lb, P["sgu_ws"][j], jnp.transpose(P["sgu_b"][j]))
            x, xn = matmul_res_norm([gated], P["w_out_odd"], j, x, g=g_ffn)
        hid, f_new = ffn_up(xn, P["ffn_w_up"], layer, P["ffn_conv_w"][layer], P["ffn_conv_b"][layer].reshape(1, D_FF),
                            _pad_state(ffn_buf[layer]), seq)
        f_l.append(f_new)
        last = layer == depth - 1
        g_next = P["norm_final"] if last else P["norm_mix"][layer + 1]
        x, xn = matmul_res_norm([hid], W["down"], layer, x, g=g_next, xn_dtype=F32 if last else BF16,
                                emit_x=not last, tm=256)
    y = xn.reshape(nb, seq, d)
    return (y, jnp.stack(conv_l), jnp.stack(h_l), jnp.stack(c_l), jnp.stack(n_l), jnp.stack(m_l), v_l,
            jnp.stack(f_l))


def kernel(x_prompt, x_sample, state_conv_mix, state_rglru_h, state_mlstm_C, state_mlstm_n, state_mlstm_m, state_ffn_conv, norm_mix, norm_ffn, norm_final, w_in_even, conv_even_w, conv_even_b, rg_wa, rg_ba, rg_wx, rg_bx, rg_lambda, ml_gate_b, ml_norm_g, w_out_even, w_in_odd, sgu_ln_g, sgu_ln_b, sgu_ws, sgu_b, w_out_odd, ffn_w_up, ffn_conv_w, ffn_conv_b, ffn_w_down):
    P = {
        "norm_mix": norm_mix, "norm_ffn": norm_ffn, "norm_final": norm_final,
        "w_in_even": w_in_even, "conv_even_w": conv_even_w, "conv_even_b": conv_even_b,
        "rg_wa": rg_wa, "rg_ba": rg_ba, "rg_wx": rg_wx, "rg_bx": rg_bx, "rg_lambda": rg_lambda,
        "ml_gate_b": ml_gate_b, "ml_norm_g": ml_norm_g, "w_out_even": w_out_even,
        "w_in_odd": w_in_odd, "sgu_ln_g": sgu_ln_g, "sgu_ln_b": sgu_ln_b, "sgu_ws": sgu_ws,
        "sgu_b": sgu_b, "w_out_odd": w_out_odd, "ffn_w_up": ffn_w_up, "ffn_conv_w": ffn_conv_w,
        "ffn_conv_b": ffn_conv_b, "ffn_w_down": ffn_w_down,
    }
    W = _prep_weights(P)
    n_even, depth = w_in_even.shape[0], ffn_w_up.shape[0]
    bp = x_prompt.shape[0]
    dt = x_prompt.dtype
    z_conv = jnp.zeros((n_even, bp, CONV_K - 1, CONV_W), dt)
    z_h = jnp.zeros((n_even, bp, D_RG), dt)
    z_c = jnp.zeros((n_even, bp, ML_HEADS, ML_DK, ML_DV), dt)
    z_n = jnp.zeros((n_even, bp, ML_HEADS, ML_DK), dt)
    z_m = jnp.zeros((n_even, bp, ML_HEADS), dt)
    z_f = jnp.zeros((depth, bp, FFN_K - 1, D_FF), dt)
    y_p, conv_p, h_p, c_p, n_p, m_p, _, f_p = _trunk(x_prompt, z_conv, z_h, z_c, z_n, z_m, z_f, P, W, short=False)
    y_s, conv_s, h_s, c_s, n_s, m_s, v_list, f_s = _trunk(
        x_sample, state_conv_mix, state_rglru_h, state_mlstm_C, state_mlstm_n, state_mlstm_m, state_ffn_conv,
        P, W, short=True)
    chunk_v_s = jnp.stack(v_list)
    return (y_p, y_s, conv_p, conv_s, h_p, h_s, c_p, c_s, n_p, n_s, m_p, m_s, chunk_v_s, f_p, f_s)
```

```python
import functools

import jax
import jax.numpy as jnp
from jax import lax
from jax.experimental import pallas as pl
from jax.experimental.pallas import tpu as pltpu

F32 = jnp.float32
BF16 = jnp.bfloat16

D_MODEL = 2048
EPS = 1e-6
D_RG = D_MODEL // 2
RG_BLOCKS = 16
RG_BW = D_RG // RG_BLOCKS
RG_C = 8.0
CONV_K = 4
ML_HEADS = 4
ML_DK = D_MODEL // 16
ML_DV = D_MODEL // 8
ML_HK = ML_HEADS * ML_DK
ML_HV = ML_HEADS * ML_DV
CONV_W = D_RG + 2 * ML_HK
E_MAIN = CONV_W + D_RG + 2 * ML_HV
D_C = D_MODEL
C_GROUPS = 8
C_GW = D_C // C_GROUPS
C_CHUNK = 128
D_FF = 5632
FFN_K = 3

SUBLANES = 8
LANES = 128
HALO = SUBLANES
RG_CB = 256
VMEM_LIMIT = 56 * 1024 * 1024
ROW_CHUNK = 256


def _params(*sem):
    return pltpu.CompilerParams(dimension_semantics=sem, vmem_limit_bytes=VMEM_LIMIT)


def _rmsnorm_kernel(x_ref, g_ref, o_ref):
    x = x_ref[...]
    ms = jnp.mean(x * x, axis=-1, keepdims=True)
    o_ref[...] = (x * lax.rsqrt(ms + EPS) * g_ref[...]).astype(o_ref.dtype)


def rmsnorm(x, g, out_dtype, tm=512):
    m, d = x.shape
    return pl.pallas_call(
        _rmsnorm_kernel,
        grid=(m // tm,),
        in_specs=[pl.BlockSpec((tm, d), lambda i: (i, 0)), pl.BlockSpec((1, d), lambda i: (0, 0))],
        out_specs=pl.BlockSpec((tm, d), lambda i: (i, 0)),
        out_shape=jax.ShapeDtypeStruct((m, d), out_dtype),
        compiler_params=_params("parallel"),
        name="rmsnorm",
    )(x, g.reshape(1, d))


def _row_chunk(tm):
    return min(tm, ROW_CHUNK)


def _load_weight(w_ref, ws_ref, first):
    @pl.when(first)
    def _():
        ws_ref[...] = w_ref[0].astype(BF16)


def _mm_nt_kernel(a_ref, w_ref, o_ref):
    tm = a_ref.shape[0]
    rc = _row_chunk(tm)
    for c in range(tm // rc):
        rs = slice(c * rc, (c + 1) * rc)
        o_ref[rs, :] = jnp.dot(a_ref[rs, :], w_ref[0], preferred_element_type=F32).astype(o_ref.dtype)


def matmul_nt(a, w, layer, n_out, col0=0, tm=1024, tn=1024):
    m, k = a.shape
    tm, tn = min(tm, m), min(tn, n_out)
    return pl.pallas_call(
        _mm_nt_kernel,
        grid=(n_out // tn, m // tm),
        in_specs=[
            pl.BlockSpec((tm, k), lambda j, i: (i, 0)),
            pl.BlockSpec((1, k, tn), lambda j, i: (layer, 0, col0 + j)),
        ],
        out_specs=pl.BlockSpec((tm, tn), lambda j, i: (i, j)),
        out_shape=jax.ShapeDtypeStruct((m, n_out), F32),
        compiler_params=_params("parallel", "arbitrary"),
        name="matmul_nt",
    )(a, w)


def _mm_full_kernel(*refs, ka, cast_w, norm, emit_x):
    n_a = len(ka)
    a_refs, w_ref, res_ref = refs[:n_a], refs[n_a], refs[n_a + 1]
    pos = n_a + 2
    g_ref = refs[pos] if norm else None
    pos += int(norm)
    x_ref = refs[pos] if emit_x else None
    pos += int(emit_x)
    xn_ref = refs[pos] if norm else None
    if cast_w:
        ws_ref = refs[-1]
        _load_weight(w_ref, ws_ref, pl.program_id(0) == 0)
        wget = lambda lo, hi: ws_ref[lo:hi, :]
    else:
        wget = lambda lo, hi: w_ref[0, lo:hi, :]
    tm = res_ref.shape[0]
    rc = _row_chunk(tm)
    for c in range(tm // rc):
        rs = slice(c * rc, (c + 1) * rc)
        y = res_ref[rs, :]
        off = 0
        for a_ref, k in zip(a_refs, ka):
            y = y + jnp.dot(a_ref[rs, :], wget(off, off + k), preferred_element_type=F32)
            off += k
        if emit_x:
            x_ref[rs, :] = y
        if norm:
            ms = jnp.mean(y * y, axis=-1, keepdims=True)
            xn_ref[rs, :] = (y * lax.rsqrt(ms + EPS) * g_ref[...]).astype(xn_ref.dtype)


def matmul_res_norm(a_list, w, layer, res, g=None, xn_dtype=BF16, emit_x=True, tm=512):
    m, n = res.shape
    ka = tuple(a.shape[1] for a in a_list)
    kt = sum(ka)
    cast_w = w.dtype != BF16
    norm = g is not None
    row = lambda width: pl.BlockSpec((tm, width), lambda i: (i, 0))
    in_specs = [row(k) for k in ka]
    in_specs.append(pl.BlockSpec((1, kt, n), lambda i: (layer, 0, 0), pipeline_mode=pl.Buffered(1)))
    in_specs.append(row(n))
    args = list(a_list) + [w, res]
    if norm:
        in_specs.append(pl.BlockSpec((1, n), lambda i: (0, 0)))
        args.append(g.reshape(1, n))
    out_specs, out_shape = [], []
    if emit_x:
        out_specs.append(row(n))
        out_shape.append(jax.ShapeDtypeStruct((m, n), F32))
    if norm:
        out_specs.append(row(n))
        out_shape.append(jax.ShapeDtypeStruct((m, n), xn_dtype))
    outs = pl.pallas_call(
        functools.partial(_mm_full_kernel, ka=ka, cast_w=cast_w, norm=norm, emit_x=emit_x),
        grid=(m // tm,),
        in_specs=in_specs,
        out_specs=out_specs,
        out_shape=out_shape,
        scratch_shapes=[pltpu.VMEM((kt, n), BF16)] if cast_w else [],
        compiler_params=_params("arbitrary"),
        name="matmul_res_norm",
    )(*args)
    x_new = outs[0] if emit_x else None
    xn = outs[-1] if norm else None
    return x_new, xn


def _halo_init(xs_ref, buf_ref, first):
    k = buf_ref.shape[1]

    @pl.when(first)
    def _():
        xs_ref[:, HALO - k:HALO, :] = buf_ref[...]


def _conv_chunk(xs_ref, x3, w_ref, b_ref, taps, seqs, r0):
    n = x3.shape[1]
    xs_ref[seqs, HALO + r0:HALO + r0 + n, :] = x3
    y = b_ref[...] + w_ref[taps - 1:taps, :] * x3
    for k in range(taps - 1):
        lo = HALO + r0 - (taps - 1) + k
        y = y + w_ref[k:k + 1, :] * xs_ref[seqs, lo:lo + n, :]
    return y


def _halo_carry(xs_ref, rows):
    xs_ref[:, 0:HALO, :] = xs_ref[:, rows:rows + HALO, :]


def _causal_conv(xs_ref, x3, buf_ref, w_ref, b_ref, first, rows, taps, carry):
    _halo_init(xs_ref, buf_ref, first)
    y = _conv_chunk(xs_ref, x3, w_ref, b_ref, taps, slice(None), 0)
    if carry:
        _halo_carry(xs_ref, rows)
    return y


def _ffn_up_kernel(*refs, ns, rows, tps, layer):
    a_ref, wg_ref, wu_ref, cw_ref, cb_ref, buf_ref = refs[:6]
    hid_ref, st_all_ref, wgs_ref, wus_ref, gs_ref = refs[-5:]
    st_ref = _own_slice(st_all_ref, layer, None)
    i = pl.program_id(1)
    _load_weight(wg_ref, wgs_ref, i == 0)
    _load_weight(wu_ref, wus_ref, i == 0)
    _halo_init(gs_ref, buf_ref, (i % tps) == 0)
    tn = wgs_ref.shape[1]
    tm = ns * rows
    rc = _row_chunk(tm)
    for c in range(tm // rc):
        a = a_ref[c * rc:(c + 1) * rc, :]
        if ns == 1:
            seqs, r0, shape = slice(0, 1), c * rc, (1, rc, tn)
        else:
            nsc = rc // rows
            seqs, r0, shape = slice(c * nsc, (c + 1) * nsc), 0, (nsc, rows, tn)
        g3 = jnp.dot(a, wgs_ref[...], preferred_element_type=F32).reshape(shape)
        u3 = jnp.dot(a, wus_ref[...], preferred_element_type=F32).reshape(shape)
        gc = _conv_chunk(gs_ref, g3, cw_ref, cb_ref, FFN_K, seqs, r0)
        hid_ref[c * rc:(c + 1) * rc, :] = (jax.nn.gelu(gc) * u3).reshape(rc, tn).astype(hid_ref.dtype)
    st_ref[...] = gs_ref[:, HALO + rows - (FFN_K - 1):HALO + rows, :]
    if tps > 1:
        _halo_carry(gs_ref, rows)


def ffn_up(xn, w_up, layer, cw, cb, buf, st_prev, seq, tm=1024, tn=512):
    m, d = xn.shape
    nb = m // seq
    depth = w_up.shape[0]
    tm = min(tm, m)
    rows = min(seq, tm)
    ns = tm // rows
    tps = seq // rows
    nj = D_FF // tn
    kern = functools.partial(_ffn_up_kernel, ns=ns, rows=rows, tps=tps, layer=layer)
    st_blk = (ns, FFN_K - 1, tn)
    in_specs = [
        pl.BlockSpec((tm, d), lambda j, i: (i, 0)),
        pl.BlockSpec((1, d, tn), lambda j, i: (layer, 0, j)),
        pl.BlockSpec((1, d, tn), lambda j, i: (layer, 0, nj + j)),
        pl.BlockSpec((FFN_K, tn), lambda j, i: (0, j)),
        pl.BlockSpec((1, tn), lambda j, i: (0, j)),
        pl.BlockSpec((None,) + st_blk, lambda j, i: (layer, i // tps, 0, j)),
    ]
    args = [xn, w_up, w_up, cw, cb, buf]
    aliases = {}
    if layer > 0:
        in_specs.append(pl.BlockSpec(memory_space=pl.ANY))
        args.append(st_prev)
        aliases = {len(args) - 1: 1}
    return pl.pallas_call(
        kern,
        grid=(nj, m // tm),
        in_specs=in_specs,
        out_specs=[
            pl.BlockSpec((tm, tn), lambda j, i: (i, j)),
            _stacked_out_spec(depth, layer, st_blk, lambda j, i: (i // tps, 0, j)),
        ],
        out_shape=[
            jax.ShapeDtypeStruct((m, D_FF), BF16),
            jax.ShapeDtypeStruct((depth, nb, FFN_K - 1, D_FF), F32),
        ],
        scratch_shapes=[pltpu.VMEM((d, tn), BF16), pltpu.VMEM((d, tn), BF16),
                        pltpu.VMEM((ns, HALO + rows, tn), F32)],
        input_output_aliases=aliases,
        compiler_params=_params("parallel", "arbitrary"),
        name="ffn_up",
    )(*args)


def _seq_scan(a, b, h0, ns, rows):
    c = a.shape[1]
    groups = rows // SUBLANES
    assert ns == 1 or groups == 1
    a3 = a.reshape(ns * groups, SUBLANES, c)
    b3 = b.reshape(ns * groups, SUBLANES, c)
    pos = lax.broadcasted_iota(jnp.int32, a3.shape, 1)
    d = 1
    while d < SUBLANES:
        m = pos >= d
        a_sh = jnp.where(m, pltpu.roll(a3, d, 1), 1.0)
        b_sh = jnp.where(m, pltpu.roll(b3, d, 1), 0.0)
        b3 = a3 * b_sh + b3
        a3 = a3 * a_sh
        d *= 2
    if groups == 1:
        return (a3 * h0 + b3).reshape(ns * rows, c)
    h_prev = h0[0]
    hs = []
    for g in range(groups):
        hg = a3[g] * h_prev + b3[g]
        hs.append(hg)
        h_prev = hg[SUBLANES - 1:SUBLANES, :]
    return jnp.concatenate(hs, axis=0)


def _rglru_kernel(x_ref, gate_ref, cw_ref, cb_ref, wa_ref, wx_ref, ba_ref, bx_ref, lam_ref, buf_ref, h0_ref,
                  out_ref, cst_ref, hl_ref, xs_ref, *, ns, rows, tps):
    c = x_ref.shape[1]
    first = pl.program_id(2) == 0

    @pl.when(first)
    def _():
        hl_ref[...] = h0_ref[...]

    x3 = x_ref[...].reshape(ns, rows, c)
    xc = _causal_conv(xs_ref, x3, buf_ref, cw_ref, cb_ref, first, rows, CONV_K, tps > 1)
    cst_ref[...] = x3[:, rows - (CONV_K - 1):rows, :]
    xc = xc.reshape(ns * rows, c)
    xb = xc.astype(BF16)
    r = jax.nn.sigmoid(jnp.dot(xb, wa_ref[0], preferred_element_type=F32) + ba_ref[...])
    i = jax.nn.sigmoid(jnp.dot(xb, wx_ref[0], preferred_element_type=F32) + bx_ref[...])
    nl = -lam_ref[...]
    softplus = jnp.maximum(nl, 0.0) + jnp.log1p(jnp.exp(-jnp.abs(nl)))
    log_a = (-RG_C) * r * softplus
    th = jnp.tanh(log_a)
    u = jnp.sqrt(-2.0 * th / (1.0 - th)) * (i * xc)
    h = _seq_scan(jnp.exp(log_a), u, hl_ref[...], ns, rows)
    hl_ref[...] = h.reshape(ns, rows, c)[:, rows - 1:rows, :]
    out_ref[...] = (jax.nn.gelu(gate_ref[...]) * h).astype(out_ref.dtype)


def rglru(proj, cw, cb, wa_bd, wx_bd, ba, bx, lam, buf, layer, h0, seq, rows, ns):
    m = proj.shape[0]
    nb = m // seq
    tps = seq // rows
    tm = ns * rows
    c = RG_CB
    gate0 = CONV_W // c
    kern = functools.partial(_rglru_kernel, ns=ns, rows=rows, tps=tps)
    vec = pl.BlockSpec((1, c), lambda b, cb_, t: (0, cb_))
    return pl.pallas_call(
        kern,
        grid=(nb // ns, D_RG // c, tps),
        in_specs=[
            pl.BlockSpec((tm, c), lambda b, cb_, t: (b * tps + t, cb_)),
            pl.BlockSpec((tm, c), lambda b, cb_, t: (b * tps + t, gate0 + cb_)),
            pl.BlockSpec((CONV_K, c), lambda b, cb_, t: (0, cb_)),
            vec,
            pl.BlockSpec((1, c, c), lambda b, cb_, t: (cb_, 0, 0)),
            pl.BlockSpec((1, c, c), lambda b, cb_, t: (cb_, 0, 0)),
            vec, vec, vec,
            pl.BlockSpec((None, ns, CONV_K - 1, c), lambda b, cb_, t: (layer, b, 0, cb_)),
            pl.BlockSpec((ns, 1, c), lambda b, cb_, t: (b, 0, cb_)),
        ],
        out_specs=[
            pl.BlockSpec((tm, c), lambda b, cb_, t: (b * tps + t, cb_)),
            pl.BlockSpec((ns, CONV_K - 1, c), lambda b, cb_, t: (b, 0, cb_)),
            pl.BlockSpec((ns, 1, c), lambda b, cb_, t: (b, 0, cb_)),
        ],
        out_shape=[
            jax.ShapeDtypeStruct((m, D_RG), BF16),
            jax.ShapeDtypeStruct((nb, CONV_K - 1, D_RG), F32),
            jax.ShapeDtypeStruct((nb, 1, D_RG), F32),
        ],
        scratch_shapes=[pltpu.VMEM((ns, HALO + rows, c), F32)],
        compiler_params=_params("parallel", "parallel", "arbitrary"),
        name="rglru",
    )(proj, proj, cw, cb, wa_bd, wx_bd, ba, bx, lam, buf, h0)


def _heads(x3, width, ns, rows):
    parts = [x3[:, None, :, h * width:(h + 1) * width] for h in range(ML_HEADS)]
    return jnp.concatenate(parts, axis=1).reshape(ns * ML_HEADS, rows, width)


def _own_slice(all_ref, layer, first):
    if layer > 0:
        return all_ref
    n_layers = all_ref.shape[0]

    def fill():
        all_ref[1:] = jnp.zeros((n_layers - 1,) + all_ref.shape[1:], all_ref.dtype)

    if n_layers > 1:
        if first is None:
            fill()
        else:
            pl.when(first)(fill)
    return all_ref.at[0]


def _mlstm_kernel(*refs, ns, rows, tps, layer):
    qk_ref, v_ref, o_ref, g_ref, cw_ref, cb_ref, gb_ref, ng_ref, buf_ref, c0_ref, n0_ref, m0_ref = refs[:12]
    out_ref, cst_ref, c_all_ref, n_ref, m_ref, xs_ref = refs[-6:]
    G = ns * ML_HEADS
    first = pl.program_id(1) == 0
    c_ref = _own_slice(c_all_ref, layer, first)

    @pl.when(first)
    def _():
        c_ref[...] = c0_ref[...]
        n_ref[...] = n0_ref[...]
        m_ref[...] = m0_ref[...]

    x3 = qk_ref[...].reshape(ns, rows, 2 * ML_HK)
    xc = _causal_conv(xs_ref, x3, buf_ref, cw_ref, cb_ref, first, rows, CONV_K, tps > 1)
    cst_ref[...] = x3[:, rows - (CONV_K - 1):rows, :]
    qk = jax.nn.silu(xc)
    q = _heads(qk[:, :, :ML_HK], ML_DK, ns, rows)
    k = _heads(qk[:, :, ML_HK:], ML_DK, ns, rows) * (ML_DK ** -0.5)
    v = _heads(v_ref[...].reshape(ns, rows, ML_HV), ML_DV, ns, rows)
    gates = g_ref[...].reshape(ns, rows, LANES) + gb_ref[...]
    i_col = _heads(gates[:, :, :ML_HEADS], 1, ns, rows)
    f_col = _heads(jax.nn.log_sigmoid(gates[:, :, ML_HEADS:2 * ML_HEADS]), 1, ns, rows)

    t_idx = lax.broadcasted_iota(jnp.int32, (rows, rows), 0)
    s_idx = lax.broadcasted_iota(jnp.int32, (rows, rows), 1)
    eye = t_idx == s_idx
    causal = t_idx >= s_idx
    i_row = jnp.sum(jnp.where(eye, i_col, 0.0), axis=1, keepdims=True)
    f_row = jnp.sum(jnp.where(eye, f_col, 0.0), axis=1, keepdims=True)
    b_col = jnp.sum(jnp.where(causal, f_row, 0.0), axis=2, keepdims=True)
    b_row = jnp.sum(jnp.where(t_idx <= s_idx, f_col, 0.0), axis=1, keepdims=True)
    dmat = jnp.where(causal, b_col - b_row + i_row, -jnp.inf)

    m_prev = m_ref[...].reshape(G, 1, LANES)[:, :, 0:1]
    inter = b_col + m_prev
    m_t = jnp.maximum(inter, jnp.max(dmat, axis=2, keepdims=True))
    qb, kb, vb = q.astype(BF16), k.astype(BF16), v.astype(BF16)
    s = jnp.einsum("gld,gsd->gls", qb, kb, preferred_element_type=F32) * jnp.exp(dmat - m_t)
    w_inter = jnp.exp(inter - m_t)
    c_old = c_ref[...].reshape(G, ML_DK, ML_DV)
    n_old = n_ref[...].reshape(G, 1, ML_DK)
    num = (w_inter * jnp.einsum("gld,gde->gle", qb, c_old.astype(BF16), preferred_element_type=F32)
           + jnp.einsum("gls,gse->gle", s.astype(BF16), vb, preferred_element_type=F32))
    den = w_inter * jnp.sum(q * n_old, axis=2, keepdims=True) + jnp.sum(s, axis=2, keepdims=True)
    h = num / jnp.maximum(jnp.abs(den), jnp.exp(-m_t))

    m_new = m_t[:, rows - 1:rows, :]
    b_last = b_col[:, rows - 1:rows, :]
    w_state = jnp.exp(b_last - b_col + i_col - m_new)
    decay = jnp.exp(b_last + m_prev - m_new)
    kw = k * w_state
    kwt = jnp.swapaxes(kw, 1, 2).astype(BF16)
    c_new = decay * c_old + jnp.einsum("gds,gse->gde", kwt, vb, preferred_element_type=F32)
    n_new = decay * n_old + jnp.sum(kw, axis=1, keepdims=True)
    c_ref[...] = c_new.reshape(ns, ML_HEADS, ML_DK, ML_DV)
    n_ref[...] = n_new.reshape(ns, ML_HEADS, 1, ML_DK)
    m_ref[...] = jnp.broadcast_to(m_new, (G, 1, LANES)).reshape(ns, ML_HEADS, 1, LANES)

    hn = h * lax.rsqrt(jnp.mean(h * h, axis=2, keepdims=True) + EPS)
    hn = hn.reshape(ns, ML_HEADS, rows, ML_DV)
    o3 = o_ref[...].reshape(ns, rows, ML_HV)
    for hh in range(ML_HEADS):
        sl = slice(hh * ML_DV, (hh + 1) * ML_DV)
        y = hn[:, hh] * ng_ref[:, sl] * jax.nn.sigmoid(o3[:, :, sl])
        out_ref[:, sl] = y.reshape(ns * rows, ML_DV).astype(out_ref.dtype)


def _stacked_out_spec(n_layers, layer, blk, idx):
    if layer == 0:
        return pl.BlockSpec((n_layers,) + blk, lambda *g: (0,) + idx(*g))
    return pl.BlockSpec((None,) + blk, lambda *g: (layer,) + idx(*g))


def mlstm(proj, gates, cw, cb, gbias, ng, buf, c0, layer, n0, m0, c_prev, seq, rows, ns):
    m = proj.shape[0]
    nb = m // seq
    n_layers = c0.shape[0]
    tps = seq // rows
    tm = ns * rows
    w = 2 * ML_HK
    kern = functools.partial(_mlstm_kernel, ns=ns, rows=rows, tps=tps, layer=layer)
    row_blk = lambda col: pl.BlockSpec((tm, w), lambda b, t: (b * tps + t, col))
    st4 = lambda *shape: pl.BlockSpec((ns,) + shape, lambda b, t: (b, 0, 0, 0))
    c_blk = (ns, ML_HEADS, ML_DK, ML_DV)
    in_specs = [
        row_blk(D_RG // w),
        row_blk((CONV_W + D_RG) // w),
        row_blk((CONV_W + D_RG + ML_HV) // w),
        pl.BlockSpec((tm, LANES), lambda b, t: (b * tps + t, 0)),
        pl.BlockSpec((CONV_K, w), lambda b, t: (0, D_RG // w)),
        pl.BlockSpec((1, w), lambda b, t: (0, D_RG // w)),
        pl.BlockSpec((1, LANES), lambda b, t: (0, 0)),
        pl.BlockSpec((1, ML_HV), lambda b, t: (0, 0)),
        pl.BlockSpec((None, ns, CONV_K - 1, w), lambda b, t: (layer, b, 0, D_RG // w)),
        pl.BlockSpec((None,) + c_blk, lambda b, t: (layer, b, 0, 0, 0)),
        st4(ML_HEADS, 1, ML_DK),
        st4(ML_HEADS, 1, LANES),
    ]
    args = [proj, proj, proj, gates, cw, cb, gbias, ng, buf, c0, n0, m0]
    aliases = {}
    if layer > 0:
        in_specs.append(pl.BlockSpec(memory_space=pl.ANY))
        args.append(c_prev)
        aliases = {len(args) - 1: 2}
    return pl.pallas_call(
        kern,
        grid=(nb // ns, tps),
        in_specs=in_specs,
        out_specs=[
            pl.BlockSpec((tm, ML_HV), lambda b, t: (b * tps + t, 0)),
            pl.BlockSpec((ns, CONV_K - 1, w), lambda b, t: (b, 0, 0)),
            _stacked_out_spec(n_layers, layer, c_blk, lambda b, t: (b, 0, 0, 0)),
            st4(ML_HEADS, 1, ML_DK),
            st4(ML_HEADS, 1, LANES),
        ],
        out_shape=[
            jax.ShapeDtypeStruct((m, ML_HV), BF16),
            jax.ShapeDtypeStruct((nb, CONV_K - 1, w), F32),
            jax.ShapeDtypeStruct((n_layers, nb) + c_blk[1:], F32),
            jax.ShapeDtypeStruct((nb, ML_HEADS, 1, ML_DK), F32),
            jax.ShapeDtypeStruct((nb, ML_HEADS, 1, LANES), F32),
        ],
        scratch_shapes=[pltpu.VMEM((ns, HALO + rows, w), F32)],
        input_output_aliases=aliases,
        compiler_params=_params("parallel", "arbitrary"),
        name="mlstm",
    )(*args)


def _layernorm(v, g_ref, b_ref):
    mu = jnp.mean(v, axis=-1, keepdims=True)
    vc = v - mu
    var = jnp.mean(vc * vc, axis=-1, keepdims=True)
    return vc * lax.rsqrt(var + EPS) * g_ref[...] + b_ref[...]


def _odd_in_proj(a, w_ref):
    u = jax.nn.gelu(jnp.dot(a, w_ref[0, :, 0:D_C], preferred_element_type=F32))
    v = jax.nn.gelu(jnp.dot(a, w_ref[0, :, D_C:2 * D_C], preferred_element_type=F32))
    return u, v


def _odd_long_kernel(a_ref, w_ref, lg_ref, lb_ref, ws_ref, sb_ref, out_ref):
    for c in range(a_ref.shape[0] // C_CHUNK):
        rs = slice(c * C_CHUNK, (c + 1) * C_CHUNK)
        u, v = _odd_in_proj(a_ref[rs, :], w_ref)
        vn = _layernorm(v, lg_ref, lb_ref).astype(BF16)
        for g in range(C_GROUPS):
            sl = slice(g * C_GW, (g + 1) * C_GW)
            mix = jnp.dot(ws_ref[0, g], vn[:, sl], preferred_element_type=F32) + sb_ref[:, g:g + 1]
            out_ref[rs, sl] = (u[:, sl] * mix).astype(out_ref.dtype)


def _odd_short_kernel(a_ref, w_ref, lg_ref, lb_ref, we_ref, be_ref, out_ref, vn_ref, *, rows):
    rc = C_CHUNK
    nsc = rc // rows
    for c in range(a_ref.shape[0] // rc):
        rs = slice(c * rc, (c + 1) * rc)
        u, v = _odd_in_proj(a_ref[rs, :], w_ref)
        vn = _layernorm(v, lg_ref, lb_ref)
        vn_ref[rs, :] = vn
        v3 = vn.reshape(nsc, rows, D_C)
        mix = jnp.broadcast_to(be_ref[...], (nsc, rows, D_C))
        for s in range(rows):
            mix = mix + we_ref[s] * v3[:, s:s + 1, :]
        out_ref[rs, :] = (u * mix.reshape(rc, D_C)).astype(out_ref.dtype)


def odd_mixer(xn, w, layer, lg, lb, mix_w, mix_b, rows=None, tm=512):
    m, d = xn.shape
    tm = min(tm, m)
    row = lambda width: pl.BlockSpec((tm, width), lambda i: (i, 0))
    vec = pl.BlockSpec((1, D_C), lambda i: (0, 0))
    in_specs = [row(d), pl.BlockSpec((1, d, 2 * D_C), lambda i: (layer, 0, 0), pipeline_mode=pl.Buffered(1)), vec, vec]
    if rows is None:
        kern = _odd_long_kernel
        in_specs += [pl.BlockSpec((1, C_GROUPS, C_CHUNK, C_CHUNK), lambda i: (layer, 0, 0, 0)),
                     pl.BlockSpec((C_CHUNK, C_GROUPS), lambda i: (0, 0))]
        out_specs = [row(D_C)]
        out_shape = [jax.ShapeDtypeStruct((m, D_C), BF16)]
    else:
        kern = functools.partial(_odd_short_kernel, rows=rows)
        in_specs += [pl.BlockSpec((rows, rows, D_C), lambda i: (0, 0, 0)), pl.BlockSpec((rows, D_C), lambda i: (0, 0))]
        out_specs = [row(D_C), row(D_C)]
        out_shape = [jax.ShapeDtypeStruct((m, D_C), BF16), jax.ShapeDtypeStruct((m, D_C), F32)]
    return pl.pallas_call(
        kern,
        grid=(m // tm,),
        in_specs=in_specs,
        out_specs=out_specs,
        out_shape=out_shape,
        compiler_params=_params("arbitrary"),
        name="odd_mixer",
    )(xn, w, lg, lb, mix_w, mix_b)


def _block_diag(w):
    per = RG_CB // RG_BW
    w = w.reshape(D_RG // RG_CB, per, RG_BW, RG_BW)
    eye = jnp.eye(per, dtype=w.dtype)
    bd = w[:, :, :, None, :] * eye[None, :, None, :, None]
    return bd.reshape(D_RG // RG_CB, RG_CB, RG_CB).astype(BF16)


def _prep_weights(P):
    n_even = P["w_in_even"].shape[0]
    pad = LANES - 2 * ML_HEADS
    return {
        "in_even": jnp.pad(P["w_in_even"], ((0, 0), (0, 0), (0, pad))).astype(BF16),
        "wa": [_block_diag(P["rg_wa"][j]) for j in range(n_even)],
        "wx": [_block_diag(P["rg_wx"][j]) for j in range(n_even)],
        "gbias": [jnp.pad(P["ml_gate_b"][j].reshape(1, 2 * ML_HEADS), ((0, 0), (0, pad))) for j in range(n_even)],
        "in_odd": P["w_in_odd"].astype(BF16),
        "sgu_ws": jnp.tril(P["sgu_ws"]).astype(BF16),
        "down": P["ffn_w_down"].astype(BF16),
    }


def _sgu_short_weights(ws, sb, rows):
    w = jnp.tril(ws)[:, :rows, :rows]
    w_exp = jnp.repeat(jnp.transpose(w, (2, 1, 0)), C_GW, axis=2)
    b_exp = jnp.repeat(jnp.transpose(sb[:, :rows]), C_GW, axis=1)
    return w_exp, b_exp


def _trunk(x, conv_buf, rg_h, ml_c, ml_n, ml_m, ffn_buf, P, W, short):
    nb, seq, d = x.shape
    m = nb * seq
    depth = P["ffn_w_up"].shape[0]
    x = x.reshape(m, d)
    if short:
        rg_rows, rg_ns = seq, 32
        ml_rows, ml_ns = seq, 8
    else:
        rg_rows, rg_ns = 256, 1
        ml_rows, ml_ns = 128, 1
    conv_l, h_l, n_l, m_l, v_l = [], [], [], [], []
    c_all = f_all = None
    xn = rmsnorm(x, P["norm_mix"][0], BF16)
    for layer in range(depth):
        j = layer // 2
        g_ffn = P["norm_ffn"][layer]
        if layer % 2 == 0:
            proj = matmul_nt(xn, W["in_even"], j, E_MAIN)
            gates = matmul_nt(xn, W["in_even"], j, LANES, col0=E_MAIN // LANES)
            cw, cb = P["conv_even_w"][j], P["conv_even_b"][j].reshape(1, CONV_W)
            rg_out, cst_rg, h_last = rglru(
                proj, cw, cb, W["wa"][j], W["wx"][j], P["rg_ba"][j].reshape(1, D_RG), P["rg_bx"][j].reshape(1, D_RG),
                P["rg_lambda"][j].reshape(1, D_RG), conv_buf, j, rg_h[j].reshape(nb, 1, D_RG), seq, rg_rows, rg_ns)
            ml_out, cst_ml, c_all, n_new, m_new = mlstm(
                proj, gates, cw, cb, W["gbias"][j], P["ml_norm_g"][j].reshape(1, ML_HV), conv_buf, ml_c, j,
                ml_n[j].reshape(nb, ML_HEADS, 1, ML_DK),
                jnp.broadcast_to(ml_m[j][:, :, None, None], (nb, ML_HEADS, 1, LANES)), c_all, seq, ml_rows, ml_ns)
            x, xn = matmul_res_norm([rg_out, ml_out], P["w_out_even"], j, x, g=g_ffn)
            conv_l.append(jnp.concatenate([cst_rg, cst_ml], axis=-1))
            h_l.append(h_last.reshape(nb, D_RG))
            n_l.append(n_new.reshape(nb, ML_HEADS, ML_DK))
            m_l.append(m_new[:, :, 0, 0])
        else:
            lg, lb = P["sgu_ln_g"][j].reshape(1, D_C), P["sgu_ln_b"][j].reshape(1, D_C)
            if short:
                w_exp, b_exp = _sgu_short_weights(P["sgu_ws"][j], P["sgu_b"][j], seq)
                gated, vn = odd_mixer(xn, W["in_odd"], j, lg, lb, w_exp, b_exp, rows=seq)
                v_l.append(vn.reshape(nb, seq, D_C))
            else:
                gated, = odd_mixer(xn, W["in_odd"], j, lg, lb, W["sgu_ws"], jnp.transpose(P["sgu_b"][j]))
            x, xn = matmul_res_norm([gated], P["w_out_odd"], j, x, g=g_ffn)
        hid, f_all = ffn_up(xn, P["ffn_w_up"], layer, P["ffn_conv_w"][layer], P["ffn_conv_b"][layer].reshape(1, D_FF),
                            ffn_buf, f_all, seq)
        last = layer == depth - 1
        g_next = P["norm_final"] if last else P["norm_mix"][layer + 1]
        x, xn = matmul_res_norm([hid], W["down"], layer, x, g=g_next, xn_dtype=F32 if last else BF16,
                                emit_x=not last, tm=256)
    y = xn.reshape(nb, seq, d)
    return y, jnp.stack(conv_l), jnp.stack(h_l), c_all, jnp.stack(n_l), jnp.stack(m_l), v_l, f_all


def kernel(x_prompt, x_sample, state_conv_mix, state_rglru_h, state_mlstm_C, state_mlstm_n, state_mlstm_m, state_ffn_conv, norm_mix, norm_ffn, norm_final, w_in_even, conv_even_w, conv_even_b, rg_wa, rg_ba, rg_wx, rg_bx, rg_lambda, ml_gate_b, ml_norm_g, w_out_even, w_in_odd, sgu_ln_g, sgu_ln_b, sgu_ws, sgu_b, w_out_odd, ffn_w_up, ffn_conv_w, ffn_conv_b, ffn_w_down):
    P = {
        "norm_mix": norm_mix, "norm_ffn": norm_ffn, "norm_final": norm_final,
        "w_in_even": w_in_even, "conv_even_w": conv_even_w, "conv_even_b": conv_even_b,
        "rg_wa": rg_wa, "rg_ba": rg_ba, "rg_wx": rg_wx, "rg_bx": rg_bx, "rg_lambda": rg_lambda,
        "ml_gate_b": ml_gate_b, "ml_norm_g": ml_norm_g, "w_out_even": w_out_even,
        "w_in_odd": w_in_odd, "sgu_ln_g": sgu_ln_g, "sgu_ln_b": sgu_ln_b, "sgu_ws": sgu_ws,
        "sgu_b": sgu_b, "w_out_odd": w_out_odd, "ffn_w_up": ffn_w_up, "ffn_conv_w": ffn_conv_w,
        "ffn_conv_b": ffn_conv_b, "ffn_w_down": ffn_w_down,
    }
    W = _prep_weights(P)
    n_even, depth = w_in_even.shape[0], ffn_w_up.shape[0]
    bp = x_prompt.shape[0]
    dt = x_prompt.dtype
    z_conv = jnp.zeros((n_even, bp, CONV_K - 1, CONV_W), dt)
    z_h = jnp.zeros((n_even, bp, D_RG), dt)
    z_c = jnp.zeros((n_even, bp, ML_HEADS, ML_DK, ML_DV), dt)
    z_n = jnp.zeros((n_even, bp, ML_HEADS, ML_DK), dt)
    z_m = jnp.zeros((n_even, bp, ML_HEADS), dt)
    z_f = jnp.zeros((depth, bp, FFN_K - 1, D_FF), dt)
    y_p, conv_p, h_p, c_p, n_p, m_p, _, f_p = _trunk(x_prompt, z_conv, z_h, z_c, z_n, z_m, z_f, P, W, short=False)
    y_s, conv_s, h_s, c_s, n_s, m_s, v_list, f_s = _trunk(
        x_sample, state_conv_mix, state_rglru_h, state_mlstm_C, state_mlstm_n, state_mlstm_m, state_ffn_conv,
        P, W, short=True)
    chunk_v_s = jnp.stack(v_list)
    return (y_p, y_s, conv_p, conv_s, h_p, h_s, c_p, c_s, n_p, n_s, m_p, m_s, chunk_v_s, f_p, f_s)
```

```python
import functools

import jax
import jax.numpy as jnp
from jax import lax
from jax.experimental import pallas as pl
from jax.experimental.pallas import tpu as pltpu

F32 = jnp.float32
BF16 = jnp.bfloat16

D_MODEL = 2048
EPS = 1e-6
D_RG = D_MODEL // 2
RG_BLOCKS = 16
RG_BW = D_RG // RG_BLOCKS
RG_C = 8.0
CONV_K = 4
ML_HEADS = 4
ML_DK = D_MODEL // 16
ML_DV = D_MODEL // 8
ML_HK = ML_HEADS * ML_DK
ML_HV = ML_HEADS * ML_DV
CONV_W = D_RG + 2 * ML_HK
E_MAIN = CONV_W + D_RG + 2 * ML_HV
D_C = D_MODEL
C_GROUPS = 8
C_GW = D_C // C_GROUPS
C_CHUNK = 128
D_FF = 5632
FFN_K = 3

SUBLANES = 8
LANES = 128
HALO = SUBLANES
RG_CB = 256
VMEM_LIMIT = 56 * 1024 * 1024
ROW_CHUNK = 256


def _params(*sem):
    return pltpu.CompilerParams(dimension_semantics=sem, vmem_limit_bytes=VMEM_LIMIT)


def _rmsnorm_kernel(x_ref, g_ref, o_ref):
    x = x_ref[...]
    ms = jnp.mean(x * x, axis=-1, keepdims=True)
    o_ref[...] = (x * lax.rsqrt(ms + EPS) * g_ref[...]).astype(o_ref.dtype)


def rmsnorm(x, g, out_dtype, tm=512):
    m, d = x.shape
    return pl.pallas_call(
        _rmsnorm_kernel,
        grid=(m // tm,),
        in_specs=[pl.BlockSpec((tm, d), lambda i: (i, 0)), pl.BlockSpec((1, d), lambda i: (0, 0))],
        out_specs=pl.BlockSpec((tm, d), lambda i: (i, 0)),
        out_shape=jax.ShapeDtypeStruct((m, d), out_dtype),
        compiler_params=_params("parallel"),
        name="rmsnorm",
    )(x, g.reshape(1, d))


def _row_chunk(tm):
    return min(tm, ROW_CHUNK)


def _load_weight(w_ref, ws_ref, first):
    @pl.when(first)
    def _():
        ws_ref[...] = w_ref[0].astype(BF16)


def _chunk_view(ns, rows, rc, c, width):
    if ns == 1:
        return slice(0, 1), c * rc, (1, rc, width)
    nsc = rc // rows
    return slice(c * nsc, (c + 1) * nsc), 0, (nsc, rows, width)


def _in_even_kernel(a_ref, w_ref, cw_ref, cb_ref, buf_ref, o_ref, st_ref, gs_ref, *, ns, rows, tps, n_conv):
    j, i = pl.program_id(0), pl.program_id(1)
    tm, tn = ns * rows, o_ref.shape[1]
    rc = _row_chunk(tm)

    def chunks(conv):
        for c in range(tm // rc):
            rs = slice(c * rc, (c + 1) * rc)
            acc = jnp.dot(a_ref[rs, :], w_ref[0], preferred_element_type=F32)
            if conv:
                seqs, r0, shape = _chunk_view(ns, rows, rc, c, tn)
                acc = _conv_chunk(gs_ref, acc.reshape(shape), cw_ref, cb_ref, CONV_K, seqs, r0).reshape(rc, tn)
            o_ref[rs, :] = acc

    @pl.when(j < n_conv)
    def _():
        _halo_init(gs_ref, buf_ref, (i % tps) == 0)
        chunks(True)
        st_ref[...] = gs_ref[:, HALO + rows - (CONV_K - 1):HALO + rows, :]
        if tps > 1:
            _halo_carry(gs_ref, rows)

    @pl.when(j >= n_conv)
    def _():
        chunks(False)


def in_even_proj(a, w, layer, cw, cb, buf, seq, tm=1024, tn=1024):
    m, k = a.shape
    nb = m // seq
    tm = min(tm, m)
    rows = min(seq, tm)
    ns = tm // rows
    tps = seq // rows
    n_conv = CONV_W // tn
    last = m // tm - 1
    jc = lambda j: jnp.minimum(j, n_conv - 1)
    seq_blk = lambda j, i: jnp.where(j < n_conv, i, last) // tps
    return pl.pallas_call(
        functools.partial(_in_even_kernel, ns=ns, rows=rows, tps=tps, n_conv=n_conv),
        grid=(E_MAIN // tn, m // tm),
        in_specs=[
            pl.BlockSpec((tm, k), lambda j, i: (i, 0)),
            pl.BlockSpec((1, k, tn), lambda j, i: (layer, 0, j)),
            pl.BlockSpec((CONV_K, tn), lambda j, i: (0, jc(j))),
            pl.BlockSpec((1, tn), lambda j, i: (0, jc(j))),
            pl.BlockSpec((None, ns, CONV_K - 1, tn), lambda j, i: (layer, seq_blk(j, i), 0, jc(j))),
        ],
        out_specs=[
            pl.BlockSpec((tm, tn), lambda j, i: (i, j)),
            pl.BlockSpec((ns, CONV_K - 1, tn), lambda j, i: (seq_blk(j, i), 0, jc(j))),
        ],
        out_shape=[
            jax.ShapeDtypeStruct((m, E_MAIN), F32),
            jax.ShapeDtypeStruct((nb, CONV_K - 1, CONV_W), F32),
        ],
        scratch_shapes=[pltpu.VMEM((ns, HALO + rows, tn), F32)],
        compiler_params=_params("arbitrary", "arbitrary"),
        name="in_even_proj",
    )(a, w, cw, cb, buf)


def _mm_full_kernel(*refs, ka, cast_w, norm, emit_x):
    n_a = len(ka)
    a_refs, w_ref, res_ref = refs[:n_a], refs[n_a], refs[n_a + 1]
    pos = n_a + 2
    g_ref = refs[pos] if norm else None
    pos += int(norm)
    x_ref = refs[pos] if emit_x else None
    pos += int(emit_x)
    xn_ref = refs[pos] if norm else None
    if cast_w:
        ws_ref = refs[-1]
        _load_weight(w_ref, ws_ref, pl.program_id(0) == 0)
        wget = lambda lo, hi: ws_ref[lo:hi, :]
    else:
        wget = lambda lo, hi: w_ref[0, lo:hi, :]
    tm = res_ref.shape[0]
    rc = _row_chunk(tm)
    for c in range(tm // rc):
        rs = slice(c * rc, (c + 1) * rc)
        y = res_ref[rs, :]
        off = 0
        for a_ref, k in zip(a_refs, ka):
            y = y + jnp.dot(a_ref[rs, :], wget(off, off + k), preferred_element_type=F32)
            off += k
        if emit_x:
            x_ref[rs, :] = y
        if norm:
            ms = jnp.mean(y * y, axis=-1, keepdims=True)
            xn_ref[rs, :] = (y * lax.rsqrt(ms + EPS) * g_ref[...]).astype(xn_ref.dtype)


def matmul_res_norm(a_list, w, layer, res, g=None, xn_dtype=BF16, emit_x=True, tm=512):
    m, n = res.shape
    ka = tuple(a.shape[1] for a in a_list)
    kt = sum(ka)
    cast_w = w.dtype != BF16
    norm = g is not None
    row = lambda width: pl.BlockSpec((tm, width), lambda i: (i, 0))
    in_specs = [row(k) for k in ka]
    in_specs.append(pl.BlockSpec((1, kt, n), lambda i: (layer, 0, 0), pipeline_mode=pl.Buffered(1)))
    in_specs.append(row(n))
    args = list(a_list) + [w, res]
    if norm:
        in_specs.append(pl.BlockSpec((1, n), lambda i: (0, 0)))
        args.append(g.reshape(1, n))
    out_specs, out_shape = [], []
    if emit_x:
        out_specs.append(row(n))
        out_shape.append(jax.ShapeDtypeStruct((m, n), F32))
    if norm:
        out_specs.append(row(n))
        out_shape.append(jax.ShapeDtypeStruct((m, n), xn_dtype))
    outs = pl.pallas_call(
        functools.partial(_mm_full_kernel, ka=ka, cast_w=cast_w, norm=norm, emit_x=emit_x),
        grid=(m // tm,),
        in_specs=in_specs,
        out_specs=out_specs,
        out_shape=out_shape,
        scratch_shapes=[pltpu.VMEM((kt, n), BF16)] if cast_w else [],
        compiler_params=_params("arbitrary"),
        name="matmul_res_norm",
    )(*args)
    x_new = outs[0] if emit_x else None
    xn = outs[-1] if norm else None
    return x_new, xn


def _halo_init(xs_ref, buf_ref, first):
    k = buf_ref.shape[1]

    @pl.when(first)
    def _():
        xs_ref[:, HALO - k:HALO, :] = buf_ref[...]


def _conv_chunk(xs_ref, x3, w_ref, b_ref, taps, seqs, r0):
    n = x3.shape[1]
    xs_ref[seqs, HALO + r0:HALO + r0 + n, :] = x3
    y = b_ref[...] + w_ref[taps - 1:taps, :] * x3
    for k in range(taps - 1):
        lo = HALO + r0 - (taps - 1) + k
        y = y + w_ref[k:k + 1, :] * xs_ref[seqs, lo:lo + n, :]
    return y


def _halo_carry(xs_ref, rows):
    xs_ref[:, 0:HALO, :] = xs_ref[:, rows:rows + HALO, :]


def _ffn_up_kernel(*refs, ns, rows, tps, layer, cast_down):
    a_ref, wg_ref, wu_ref, cw_ref, cb_ref, buf_ref = refs[:6]
    wgs_ref, wus_ref, gs_ref = refs[-3:]
    hid_ref, st_all_ref = refs[-5 - int(cast_down):-3 - int(cast_down)]
    st_ref = _own_slice(st_all_ref, layer, None)
    i = pl.program_id(1)
    _load_weight(wg_ref, wgs_ref, i == 0)
    _load_weight(wu_ref, wus_ref, i == 0)
    if cast_down:
        _load_weight(refs[6], refs[-4], i == 0)
    _halo_init(gs_ref, buf_ref, (i % tps) == 0)
    tn = wgs_ref.shape[1]
    tm = ns * rows
    rc = _row_chunk(tm)
    for c in range(tm // rc):
        a = a_ref[c * rc:(c + 1) * rc, :]
        seqs, r0, shape = _chunk_view(ns, rows, rc, c, tn)
        g3 = jnp.dot(a, wgs_ref[...], preferred_element_type=F32).reshape(shape)
        u3 = jnp.dot(a, wus_ref[...], preferred_element_type=F32).reshape(shape)
        gc = _conv_chunk(gs_ref, g3, cw_ref, cb_ref, FFN_K, seqs, r0)
        hid_ref[c * rc:(c + 1) * rc, :] = (jax.nn.gelu(gc) * u3).reshape(rc, tn).astype(hid_ref.dtype)
    st_ref[...] = gs_ref[:, HALO + rows - (FFN_K - 1):HALO + rows, :]
    if tps > 1:
        _halo_carry(gs_ref, rows)


def ffn_up(xn, w_up, layer, cw, cb, buf, st_prev, seq, w_down=None, tm=1024, tn=512):
    m, d = xn.shape
    nb = m // seq
    depth = w_up.shape[0]
    tm = min(tm, m)
    rows = min(seq, tm)
    ns = tm // rows
    tps = seq // rows
    nj = D_FF // tn
    cast_down = w_down is not None
    kern = functools.partial(_ffn_up_kernel, ns=ns, rows=rows, tps=tps, layer=layer, cast_down=cast_down)
    st_blk = (ns, FFN_K - 1, tn)
    in_specs = [
        pl.BlockSpec((tm, d), lambda j, i: (i, 0)),
        pl.BlockSpec((1, d, tn), lambda j, i: (layer, 0, j)),
        pl.BlockSpec((1, d, tn), lambda j, i: (layer, 0, nj + j)),
        pl.BlockSpec((FFN_K, tn), lambda j, i: (0, j)),
        pl.BlockSpec((1, tn), lambda j, i: (0, j)),
        pl.BlockSpec((None,) + st_blk, lambda j, i: (layer, i // tps, 0, j)),
    ]
    args = [xn, w_up, w_up, cw, cb, buf]
    out_specs = [
        pl.BlockSpec((tm, tn), lambda j, i: (i, j)),
        _stacked_out_spec(depth, layer, st_blk, lambda j, i: (i // tps, 0, j)),
    ]
    out_shape = [
        jax.ShapeDtypeStruct((m, D_FF), BF16),
        jax.ShapeDtypeStruct((depth, nb, FFN_K - 1, D_FF), F32),
    ]
    if cast_down:
        in_specs.append(pl.BlockSpec((1, tn, d), lambda j, i: (layer, j, 0)))
        args.append(w_down)
        out_specs.append(pl.BlockSpec((tn, d), lambda j, i: (j, 0)))
        out_shape.append(jax.ShapeDtypeStruct((D_FF, d), BF16))
    aliases = {}
    if layer > 0:
        in_specs.append(pl.BlockSpec(memory_space=pl.ANY))
        args.append(st_prev)
        aliases = {len(args) - 1: 1}
    return pl.pallas_call(
        kern,
        grid=(nj, m // tm),
        in_specs=in_specs,
        out_specs=out_specs,
        out_shape=out_shape,
        scratch_shapes=[pltpu.VMEM((d, tn), BF16), pltpu.VMEM((d, tn), BF16),
                        pltpu.VMEM((ns, HALO + rows, tn), F32)],
        input_output_aliases=aliases,
        compiler_params=_params("parallel", "arbitrary"),
        name="ffn_up",
    )(*args)


def _seq_scan(a, b, h0, ns, rows):
    c = a.shape[1]
    groups = rows // SUBLANES
    assert ns == 1 or groups == 1
    a3 = a.reshape(ns * groups, SUBLANES, c)
    b3 = b.reshape(ns * groups, SUBLANES, c)
    pos = lax.broadcasted_iota(jnp.int32, a3.shape, 1)
    d = 1
    while d < SUBLANES:
        m = pos >= d
        a_sh = jnp.where(m, pltpu.roll(a3, d, 1), 1.0)
        b_sh = jnp.where(m, pltpu.roll(b3, d, 1), 0.0)
        b3 = a3 * b_sh + b3
        a3 = a3 * a_sh
        d *= 2
    if groups == 1:
        return (a3 * h0 + b3).reshape(ns * rows, c)
    h_prev = h0[0]
    hs = []
    for g in range(groups):
        hg = a3[g] * h_prev + b3[g]
        hs.append(hg)
        h_prev = hg[SUBLANES - 1:SUBLANES, :]
    return jnp.concatenate(hs, axis=0)


def _rglru_kernel(x_ref, gate_ref, wa_ref, wx_ref, ba_ref, bx_ref, lam_ref, h0_ref, out_ref, hl_ref, *, ns, rows):
    c = x_ref.shape[1]

    @pl.when(pl.program_id(2) == 0)
    def _():
        hl_ref[...] = h0_ref[...]

    xc = x_ref[...]
    xb = xc.astype(BF16)
    r = jax.nn.sigmoid(jnp.dot(xb, wa_ref[0], preferred_element_type=F32) + ba_ref[...])
    i = jax.nn.sigmoid(jnp.dot(xb, wx_ref[0], preferred_element_type=F32) + bx_ref[...])
    nl = -lam_ref[...]
    softplus = jnp.maximum(nl, 0.0) + jnp.log1p(jnp.exp(-jnp.abs(nl)))
    log_a = (-RG_C) * r * softplus
    th = jnp.tanh(log_a)
    u = jnp.sqrt(-2.0 * th / (1.0 - th)) * (i * xc)
    h = _seq_scan(jnp.exp(log_a), u, hl_ref[...], ns, rows)
    hl_ref[...] = h.reshape(ns, rows, c)[:, rows - 1:rows, :]
    out_ref[...] = (jax.nn.gelu(gate_ref[...]) * h).astype(out_ref.dtype)


def rglru(proj, wa_bd, wx_bd, ba, bx, lam, h0, seq, rows, ns):
    m = proj.shape[0]
    nb = m // seq
    tps = seq // rows
    tm = ns * rows
    c = RG_CB
    gate0 = CONV_W // c
    kern = functools.partial(_rglru_kernel, ns=ns, rows=rows)
    vec = pl.BlockSpec((1, c), lambda b, cb_, t: (0, cb_))
    return pl.pallas_call(
        kern,
        grid=(nb // ns, D_RG // c, tps),
        in_specs=[
            pl.BlockSpec((tm, c), lambda b, cb_, t: (b * tps + t, cb_)),
            pl.BlockSpec((tm, c), lambda b, cb_, t: (b * tps + t, gate0 + cb_)),
            pl.BlockSpec((1, c, c), lambda b, cb_, t: (cb_, 0, 0)),
            pl.BlockSpec((1, c, c), lambda b, cb_, t: (cb_, 0, 0)),
            vec, vec, vec,
            pl.BlockSpec((ns, 1, c), lambda b, cb_, t: (b, 0, cb_)),
        ],
        out_specs=[
            pl.BlockSpec((tm, c), lambda b, cb_, t: (b * tps + t, cb_)),
            pl.BlockSpec((ns, 1, c), lambda b, cb_, t: (b, 0, cb_)),
        ],
        out_shape=[
            jax.ShapeDtypeStruct((m, D_RG), BF16),
            jax.ShapeDtypeStruct((nb, 1, D_RG), F32),
        ],
        compiler_params=_params("parallel", "parallel", "arbitrary"),
        name="rglru",
    )(proj, proj, wa_bd, wx_bd, ba, bx, lam, h0)


def _heads(x3, width, ns, rows):
    parts = [x3[:, None, :, h * width:(h + 1) * width] for h in range(ML_HEADS)]
    return jnp.concatenate(parts, axis=1).reshape(ns * ML_HEADS, rows, width)


def _own_slice(all_ref, layer, first):
    if layer > 0:
        return all_ref
    n_layers = all_ref.shape[0]

    def fill():
        all_ref[1:] = jnp.zeros((n_layers - 1,) + all_ref.shape[1:], all_ref.dtype)

    if n_layers > 1:
        if first is None:
            fill()
        else:
            pl.when(first)(fill)
    return all_ref.at[0]


def _mlstm_kernel(*refs, ns, rows, layer):
    qk_ref, v_ref, o_ref, xn_ref, wg_ref, gb_ref, ng_ref, c0_ref, n0_ref, m0_ref = refs[:10]
    out_ref, c_all_ref, n_ref, m_ref = refs[-4:]
    G = ns * ML_HEADS
    first = pl.program_id(1) == 0
    c_ref = _own_slice(c_all_ref, layer, first)

    @pl.when(first)
    def _():
        c_ref[...] = c0_ref[...]
        n_ref[...] = n0_ref[...]
        m_ref[...] = m0_ref[...]

    qk = jax.nn.silu(qk_ref[...].reshape(ns, rows, 2 * ML_HK))
    q = _heads(qk[:, :, :ML_HK], ML_DK, ns, rows)
    k = _heads(qk[:, :, ML_HK:], ML_DK, ns, rows) * (ML_DK ** -0.5)
    v = _heads(v_ref[...].reshape(ns, rows, ML_HV), ML_DV, ns, rows)
    gates = jnp.dot(xn_ref[...], wg_ref[0], preferred_element_type=F32).reshape(ns, rows, LANES) + gb_ref[...]
    i_col = _heads(gates[:, :, :ML_HEADS], 1, ns, rows)
    f_col = _heads(jax.nn.log_sigmoid(gates[:, :, ML_HEADS:2 * ML_HEADS]), 1, ns, rows)

    t_idx = lax.broadcasted_iota(jnp.int32, (rows, rows), 0)
    s_idx = lax.broadcasted_iota(jnp.int32, (rows, rows), 1)
    eye = t_idx == s_idx
    causal = t_idx >= s_idx
    i_row = jnp.sum(jnp.where(eye, i_col, 0.0), axis=1, keepdims=True)
    f_row = jnp.sum(jnp.where(eye, f_col, 0.0), axis=1, keepdims=True)
    b_col = jnp.sum(jnp.where(causal, f_row, 0.0), axis=2, keepdims=True)
    b_row = jnp.sum(jnp.where(t_idx <= s_idx, f_col, 0.0), axis=1, keepdims=True)
    dmat = jnp.where(causal, b_col - b_row + i_row, -jnp.inf)

    m_prev = m_ref[...].reshape(G, 1, LANES)[:, :, 0:1]
    inter = b_col + m_prev
    m_t = jnp.maximum(inter, jnp.max(dmat, axis=2, keepdims=True))
    qb, kb, vb = q.astype(BF16), k.astype(BF16), v.astype(BF16)
    s = jnp.einsum("gld,gsd->gls", qb, kb, preferred_element_type=F32) * jnp.exp(dmat - m_t)
    w_inter = jnp.exp(inter - m_t)
    c_old = c_ref[...].reshape(G, ML_DK, ML_DV)
    n_old = n_ref[...].reshape(G, 1, ML_DK)
    num = (w_inter * jnp.einsum("gld,gde->gle", qb, c_old.astype(BF16), preferred_element_type=F32)
           + jnp.einsum("gls,gse->gle", s.astype(BF16), vb, preferred_element_type=F32))
    den = w_inter * jnp.sum(q * n_old, axis=2, keepdims=True) + jnp.sum(s, axis=2, keepdims=True)
    h = num / jnp.maximum(jnp.abs(den), jnp.exp(-m_t))

    m_new = m_t[:, rows - 1:rows, :]
    b_last = b_col[:, rows - 1:rows, :]
    w_state = jnp.exp(b_last - b_col + i_col - m_new)
    decay = jnp.exp(b_last + m_prev - m_new)
    kw = k * w_state
    kwt = jnp.swapaxes(kw, 1, 2).astype(BF16)
    c_new = decay * c_old + jnp.einsum("gds,gse->gde", kwt, vb, preferred_element_type=F32)
    n_new = decay * n_old + jnp.sum(kw, axis=1, keepdims=True)
    c_ref[...] = c_new.reshape(ns, ML_HEADS, ML_DK, ML_DV)
    n_ref[...] = n_new.reshape(ns, ML_HEADS, 1, ML_DK)
    m_ref[...] = jnp.broadcast_to(m_new, (G, 1, LANES)).reshape(ns, ML_HEADS, 1, LANES)

    hn = h * lax.rsqrt(jnp.mean(h * h, axis=2, keepdims=True) + EPS)
    hn = hn.reshape(ns, ML_HEADS, rows, ML_DV)
    o3 = o_ref[...].reshape(ns, rows, ML_HV)
    for hh in range(ML_HEADS):
        sl = slice(hh * ML_DV, (hh + 1) * ML_DV)
        y = hn[:, hh] * ng_ref[:, sl] * jax.nn.sigmoid(o3[:, :, sl])
        out_ref[:, sl] = y.reshape(ns * rows, ML_DV).astype(out_ref.dtype)


def _stacked_out_spec(n_layers, layer, blk, idx):
    if layer == 0:
        return pl.BlockSpec((n_layers,) + blk, lambda *g: (0,) + idx(*g))
    return pl.BlockSpec((None,) + blk, lambda *g: (layer,) + idx(*g))


def mlstm(proj, xn, w_even, gbias, ng, c0, layer, n0, m0, c_prev, seq, rows, ns):
    assert ns == 1 or rows == seq
    m = proj.shape[0]
    nb = m // seq
    n_layers = c0.shape[0]
    tps = seq // rows
    tm = ns * rows
    w = 2 * ML_HK
    kern = functools.partial(_mlstm_kernel, ns=ns, rows=rows, layer=layer)
    row_blk = lambda col: pl.BlockSpec((tm, w), lambda b, t: (b * tps + t, col))
    st4 = lambda *shape: pl.BlockSpec((ns,) + shape, lambda b, t: (b, 0, 0, 0))
    c_blk = (ns, ML_HEADS, ML_DK, ML_DV)
    in_specs = [
        row_blk(D_RG // w),
        row_blk((CONV_W + D_RG) // w),
        row_blk((CONV_W + D_RG + ML_HV) // w),
        pl.BlockSpec((tm, D_MODEL), lambda b, t: (b * tps + t, 0)),
        pl.BlockSpec((1, D_MODEL, LANES), lambda b, t: (layer, 0, E_MAIN // LANES)),
        pl.BlockSpec((1, LANES), lambda b, t: (0, 0)),
        pl.BlockSpec((1, ML_HV), lambda b, t: (0, 0)),
        pl.BlockSpec((None,) + c_blk, lambda b, t: (layer, b, 0, 0, 0)),
        st4(ML_HEADS, 1, ML_DK),
        st4(ML_HEADS, 1, LANES),
    ]
    args = [proj, proj, proj, xn, w_even, gbias, ng, c0, n0, m0]
    aliases = {}
    if layer > 0:
        in_specs.append(pl.BlockSpec(memory_space=pl.ANY))
        args.append(c_prev)
        aliases = {len(args) - 1: 1}
    return pl.pallas_call(
        kern,
        grid=(nb // ns, tps),
        in_specs=in_specs,
        out_specs=[
            pl.BlockSpec((tm, ML_HV), lambda b, t: (b * tps + t, 0)),
            _stacked_out_spec(n_layers, layer, c_blk, lambda b, t: (b, 0, 0, 0)),
            st4(ML_HEADS, 1, ML_DK),
            st4(ML_HEADS, 1, LANES),
        ],
        out_shape=[
            jax.ShapeDtypeStruct((m, ML_HV), BF16),
            jax.ShapeDtypeStruct((n_layers, nb) + c_blk[1:], F32),
            jax.ShapeDtypeStruct((nb, ML_HEADS, 1, ML_DK), F32),
            jax.ShapeDtypeStruct((nb, ML_HEADS, 1, LANES), F32),
        ],
        input_output_aliases=aliases,
        compiler_params=_params("parallel", "arbitrary"),
        name="mlstm",
    )(*args)


def _layernorm(v, g_ref, b_ref):
    mu = jnp.mean(v, axis=-1, keepdims=True)
    vc = v - mu
    var = jnp.mean(vc * vc, axis=-1, keepdims=True)
    return vc * lax.rsqrt(var + EPS) * g_ref[...] + b_ref[...]


def _odd_in_proj(a, w_ref):
    u = jax.nn.gelu(jnp.dot(a, w_ref[0, :, 0:D_C], preferred_element_type=F32))
    v = jax.nn.gelu(jnp.dot(a, w_ref[0, :, D_C:2 * D_C], preferred_element_type=F32))
    return u, v


def _odd_long_kernel(a_ref, w_ref, lg_ref, lb_ref, ws_ref, sb_ref, out_ref):
    for c in range(a_ref.shape[0] // C_CHUNK):
        rs = slice(c * C_CHUNK, (c + 1) * C_CHUNK)
        u, v = _odd_in_proj(a_ref[rs, :], w_ref)
        vn = _layernorm(v, lg_ref, lb_ref).astype(BF16)
        for g in range(C_GROUPS):
            sl = slice(g * C_GW, (g + 1) * C_GW)
            mix = jnp.dot(ws_ref[0, g], vn[:, sl], preferred_element_type=F32) + sb_ref[:, g:g + 1]
            out_ref[rs, sl] = (u[:, sl] * mix).astype(out_ref.dtype)


def _odd_short_kernel(a_ref, w_ref, lg_ref, lb_ref, we_ref, be_ref, out_ref, vn_ref, *, rows):
    rc = C_CHUNK
    nsc = rc // rows
    for c in range(a_ref.shape[0] // rc):
        rs = slice(c * rc, (c + 1) * rc)
        u, v = _odd_in_proj(a_ref[rs, :], w_ref)
        vn = _layernorm(v, lg_ref, lb_ref)
        vn_ref[rs, :] = vn
        v3 = vn.reshape(nsc, rows, D_C)
        mix = jnp.broadcast_to(be_ref[...], (nsc, rows, D_C))
        for s in range(rows):
            mix = mix + we_ref[s] * v3[:, s:s + 1, :]
        out_ref[rs, :] = (u * mix.reshape(rc, D_C)).astype(out_ref.dtype)


def odd_mixer(xn, w, layer, lg, lb, mix_w, mix_b, rows=None, tm=1024):
    m, d = xn.shape
    tm = min(tm, m)
    row = lambda width: pl.BlockSpec((tm, width), lambda i: (i, 0))
    vec = pl.BlockSpec((1, D_C), lambda i: (0, 0))
    in_specs = [row(d), pl.BlockSpec((1, d, 2 * D_C), lambda i: (layer, 0, 0), pipeline_mode=pl.Buffered(1)), vec, vec]
    if rows is None:
        kern = _odd_long_kernel
        in_specs += [pl.BlockSpec((1, C_GROUPS, C_CHUNK, C_CHUNK), lambda i: (layer, 0, 0, 0)),
                     pl.BlockSpec((C_CHUNK, C_GROUPS), lambda i: (0, 0))]
        out_specs = [row(D_C)]
        out_shape = [jax.ShapeDtypeStruct((m, D_C), BF16)]
    else:
        kern = functools.partial(_odd_short_kernel, rows=rows)
        in_specs += [pl.BlockSpec((rows, rows, D_C), lambda i: (0, 0, 0)), pl.BlockSpec((rows, D_C), lambda i: (0, 0))]
        out_specs = [row(D_C), row(D_C)]
        out_shape = [jax.ShapeDtypeStruct((m, D_C), BF16), jax.ShapeDtypeStruct((m, D_C), F32)]
    return pl.pallas_call(
        kern,
        grid=(m // tm,),
        in_specs=in_specs,
        out_specs=out_specs,
        out_shape=out_shape,
        compiler_params=_params("arbitrary"),
        name="odd_mixer",
    )(xn, w, lg, lb, mix_w, mix_b)


def _block_diag(w):
    per = RG_CB // RG_BW
    w = w.reshape(D_RG // RG_CB, per, RG_BW, RG_BW)
    eye = jnp.eye(per, dtype=w.dtype)
    bd = w[:, :, :, None, :] * eye[None, :, None, :, None]
    return bd.reshape(D_RG // RG_CB, RG_CB, RG_CB).astype(BF16)


def _prep_weights(P):
    n_even = P["w_in_even"].shape[0]
    pad = LANES - 2 * ML_HEADS
    return {
        "in_even": jnp.pad(P["w_in_even"], ((0, 0), (0, 0), (0, pad))).astype(BF16),
        "wa": [_block_diag(P["rg_wa"][j]) for j in range(n_even)],
        "wx": [_block_diag(P["rg_wx"][j]) for j in range(n_even)],
        "gbias": [jnp.pad(P["ml_gate_b"][j].reshape(1, 2 * ML_HEADS), ((0, 0), (0, pad))) for j in range(n_even)],
        "in_odd": P["w_in_odd"].astype(BF16),
        "sgu_ws": jnp.tril(P["sgu_ws"]).astype(BF16),
    }


def _sgu_short_weights(ws, sb, rows):
    w = jnp.tril(ws)[:, :rows, :rows]
    w_exp = jnp.repeat(jnp.transpose(w, (2, 1, 0)), C_GW, axis=2)
    b_exp = jnp.repeat(jnp.transpose(sb[:, :rows]), C_GW, axis=1)
    return w_exp, b_exp


def _trunk(x, conv_buf, rg_h, ml_c, ml_n, ml_m, ffn_buf, P, W, down_bf, short):
    nb, seq, d = x.shape
    m = nb * seq
    depth = P["ffn_w_up"].shape[0]
    x = x.reshape(m, d)
    if short:
        rg_rows, rg_ns = seq, 32
        ml_rows, ml_ns = seq, 8
    else:
        rg_rows, rg_ns = 256, 1
        ml_rows, ml_ns = 128, 1
    conv_l, h_l, n_l, m_l, v_l = [], [], [], [], []
    c_all = f_all = None
    xn = rmsnorm(x, P["norm_mix"][0], BF16)
    for layer in range(depth):
        j = layer // 2
        g_ffn = P["norm_ffn"][layer]
        if layer % 2 == 0:
            proj, cst = in_even_proj(xn, W["in_even"], j, P["conv_even_w"][j], P["conv_even_b"][j].reshape(1, CONV_W),
                                     conv_buf, seq)
            rg_out, h_last = rglru(
                proj, W["wa"][j], W["wx"][j], P["rg_ba"][j].reshape(1, D_RG), P["rg_bx"][j].reshape(1, D_RG),
                P["rg_lambda"][j].reshape(1, D_RG), rg_h[j].reshape(nb, 1, D_RG), seq, rg_rows, rg_ns)
            ml_out, c_all, n_new, m_new = mlstm(
                proj, xn, W["in_even"], W["gbias"][j], P["ml_norm_g"][j].reshape(1, ML_HV), ml_c, j,
                ml_n[j].reshape(nb, ML_HEADS, 1, ML_DK),
                jnp.broadcast_to(ml_m[j][:, :, None, None], (nb, ML_HEADS, 1, LANES)), c_all, seq, ml_rows, ml_ns)
            x, xn = matmul_res_norm([rg_out, ml_out], P["w_out_even"], j, x, g=g_ffn)
            conv_l.append(cst)
            h_l.append(h_last.reshape(nb, D_RG))
            n_l.append(n_new.reshape(nb, ML_HEADS, ML_DK))
            m_l.append(m_new[:, :, 0, 0])
        else:
            lg, lb = P["sgu_ln_g"][j].reshape(1, D_C), P["sgu_ln_b"][j].reshape(1, D_C)
            if short:
                w_exp, b_exp = _sgu_short_weights(P["sgu_ws"][j], P["sgu_b"][j], seq)
                gated, vn = odd_mixer(xn, W["in_odd"], j, lg, lb, w_exp, b_exp, rows=seq)
                v_l.append(vn.reshape(nb, seq, D_C))
            else:
                gated, = odd_mixer(xn, W["in_odd"], j, lg, lb, W["sgu_ws"], jnp.transpose(P["sgu_b"][j]))
            x, xn = matmul_res_norm([gated], P["w_out_odd"], j, x, g=g_ffn)
        ffn_args = (xn, P["ffn_w_up"], layer, P["ffn_conv_w"][layer], P["ffn_conv_b"][layer].reshape(1, D_FF),
                    ffn_buf, f_all, seq)
        if len(down_bf) <= layer:
            hid, f_all, wd = ffn_up(*ffn_args, w_down=P["ffn_w_down"])
            down_bf.append(wd[None])
        else:
            hid, f_all = ffn_up(*ffn_args)
        last = layer == depth - 1
        g_next = P["norm_final"] if last else P["norm_mix"][layer + 1]
        x, xn = matmul_res_norm([hid], down_bf[layer], 0, x, g=g_next, xn_dtype=F32 if last else BF16,
                                emit_x=not last, tm=256)
    y = xn.reshape(nb, seq, d)
    return y, jnp.stack(conv_l), jnp.stack(h_l), c_all, jnp.stack(n_l), jnp.stack(m_l), v_l, f_all


def kernel(x_prompt, x_sample, state_conv_mix, state_rglru_h, state_mlstm_C, state_mlstm_n, state_mlstm_m, state_ffn_conv, norm_mix, norm_ffn, norm_final, w_in_even, conv_even_w, conv_even_b, rg_wa, rg_ba, rg_wx, rg_bx, rg_lambda, ml_gate_b, ml_norm_g, w_out_even, w_in_odd, sgu_ln_g, sgu_ln_b, sgu_ws, sgu_b, w_out_odd, ffn_w_up, ffn_conv_w, ffn_conv_b, ffn_w_down):
    P = {
        "norm_mix": norm_mix, "norm_ffn": norm_ffn, "norm_final": norm_final,
        "w_in_even": w_in_even, "conv_even_w": conv_even_w, "conv_even_b": conv_even_b,
        "rg_wa": rg_wa, "rg_ba": rg_ba, "rg_wx": rg_wx, "rg_bx": rg_bx, "rg_lambda": rg_lambda,
        "ml_gate_b": ml_gate_b, "ml_norm_g": ml_norm_g, "w_out_even": w_out_even,
        "w_in_odd": w_in_odd, "sgu_ln_g": sgu_ln_g, "sgu_ln_b": sgu_ln_b, "sgu_ws": sgu_ws,
        "sgu_b": sgu_b, "w_out_odd": w_out_odd, "ffn_w_up": ffn_w_up, "ffn_conv_w": ffn_conv_w,
        "ffn_conv_b": ffn_conv_b, "ffn_w_down": ffn_w_down,
    }
    W = _prep_weights(P)
    n_even, depth = w_in_even.shape[0], ffn_w_up.shape[0]
    bp = x_prompt.shape[0]
    dt = x_prompt.dtype
    z_conv = jnp.zeros((n_even, bp, CONV_K - 1, CONV_W), dt)
    z_h = jnp.zeros((n_even, bp, D_RG), dt)
    z_c = jnp.zeros((n_even, bp, ML_HEADS, ML_DK, ML_DV), dt)
    z_n = jnp.zeros((n_even, bp, ML_HEADS, ML_DK), dt)
    z_m = jnp.zeros((n_even, bp, ML_HEADS), dt)
    z_f = jnp.zeros((depth, bp, FFN_K - 1, D_FF), dt)
    down_bf = []
    y_p, conv_p, h_p, c_p, n_p, m_p, _, f_p = _trunk(x_prompt, z_conv, z_h, z_c, z_n, z_m, z_f, P, W, down_bf,
                                                     short=False)
    y_s, conv_s, h_s, c_s, n_s, m_s, v_list, f_s = _trunk(
        x_sample, state_conv_mix, state_rglru_h, state_mlstm_C, state_mlstm_n, state_mlstm_m, state_ffn_conv,
        P, W, down_bf, short=True)
    chunk_v_s = jnp.stack(v_list)
    return (y_p, y_s, conv_p, conv_s, h_p, h_s, c_p, c_s, n_p, n_s, m_p, m_s, chunk_v_s, f_p, f_s)
```

```python
import functools

import jax
import jax.numpy as jnp
from jax import lax
from jax.experimental import pallas as pl
from jax.experimental.pallas import tpu as pltpu

F32 = jnp.float32
BF16 = jnp.bfloat16

D_MODEL = 2048
EPS = 1e-6
D_RG = D_MODEL // 2
RG_BLOCKS = 16
RG_BW = D_RG // RG_BLOCKS
RG_C = 8.0
CONV_K = 4
ML_HEADS = 4
ML_DK = D_MODEL // 16
ML_DV = D_MODEL // 8
ML_HK = ML_HEADS * ML_DK
ML_HV = ML_HEADS * ML_DV
CONV_W = D_RG + 2 * ML_HK
E_MAIN = CONV_W + D_RG + 2 * ML_HV
D_C = D_MODEL
C_GROUPS = 8
C_GW = D_C // C_GROUPS
C_CHUNK = 128
D_FF = 5632
FFN_K = 3

SUBLANES = 8
LANES = 128
HALO = SUBLANES
RG_CB = 256
VMEM_LIMIT = 56 * 1024 * 1024
ROW_CHUNK = 256


def _params(*sem):
    return pltpu.CompilerParams(dimension_semantics=sem, vmem_limit_bytes=VMEM_LIMIT)


def _rmsnorm_kernel(x_ref, g_ref, o_ref):
    x = x_ref[...]
    ms = jnp.mean(x * x, axis=-1, keepdims=True)
    o_ref[...] = (x * lax.rsqrt(ms + EPS) * g_ref[...]).astype(o_ref.dtype)


def rmsnorm(x, g, out_dtype, tm=512):
    m, d = x.shape
    return pl.pallas_call(
        _rmsnorm_kernel,
        grid=(m // tm,),
        in_specs=[pl.BlockSpec((tm, d), lambda i: (i, 0)), pl.BlockSpec((1, d), lambda i: (0, 0))],
        out_specs=pl.BlockSpec((tm, d), lambda i: (i, 0)),
        out_shape=jax.ShapeDtypeStruct((m, d), out_dtype),
        compiler_params=_params("parallel"),
        name="rmsnorm",
    )(x, g.reshape(1, d))


def _row_chunk(tm):
    return min(tm, ROW_CHUNK)


def _load_weight(w_ref, ws_ref, first):
    @pl.when(first)
    def _():
        ws_ref[...] = w_ref[0].astype(BF16)


def _mm_nt_kernel(a_ref, w_ref, o_ref):
    tm = a_ref.shape[0]
    rc = _row_chunk(tm)
    for c in range(tm // rc):
        rs = slice(c * rc, (c + 1) * rc)
        o_ref[rs, :] = jnp.dot(a_ref[rs, :], w_ref[0], preferred_element_type=F32).astype(o_ref.dtype)


def matmul_nt(a, w, layer, n_out, tm=1024, tn=1024):
    m, k = a.shape
    tm, tn = min(tm, m), min(tn, n_out)
    return pl.pallas_call(
        _mm_nt_kernel,
        grid=(n_out // tn, m // tm),
        in_specs=[
            pl.BlockSpec((tm, k), lambda j, i: (i, 0)),
            pl.BlockSpec((1, k, tn), lambda j, i: (layer, 0, j)),
        ],
        out_specs=pl.BlockSpec((tm, tn), lambda j, i: (i, j)),
        out_shape=jax.ShapeDtypeStruct((m, n_out), F32),
        compiler_params=_params("parallel", "arbitrary"),
        name="matmul_nt",
    )(a, w)


def _mm_full_kernel(*refs, ka, cast_w, norm, emit_x):
    n_a = len(ka)
    a_refs, w_ref, res_ref = refs[:n_a], refs[n_a], refs[n_a + 1]
    pos = n_a + 2
    g_ref = refs[pos] if norm else None
    pos += int(norm)
    x_ref = refs[pos] if emit_x else None
    pos += int(emit_x)
    xn_ref = refs[pos] if norm else None
    if cast_w:
        ws_ref = refs[-1]
        _load_weight(w_ref, ws_ref, pl.program_id(0) == 0)
        wget = lambda lo, hi: ws_ref[lo:hi, :]
    else:
        wget = lambda lo, hi: w_ref[0, lo:hi, :]
    tm = res_ref.shape[0]
    rc = _row_chunk(tm)
    for c in range(tm // rc):
        rs = slice(c * rc, (c + 1) * rc)
        y = res_ref[rs, :]
        off = 0
        for a_ref, k in zip(a_refs, ka):
            y = y + jnp.dot(a_ref[rs, :], wget(off, off + k), preferred_element_type=F32)
            off += k
        if emit_x:
            x_ref[rs, :] = y
        if norm:
            ms = jnp.mean(y * y, axis=-1, keepdims=True)
            xn_ref[rs, :] = (y * lax.rsqrt(ms + EPS) * g_ref[...]).astype(xn_ref.dtype)


def matmul_res_norm(a_list, w, layer, res, g=None, xn_dtype=BF16, emit_x=True, tm=512):
    m, n = res.shape
    ka = tuple(a.shape[1] for a in a_list)
    kt = sum(ka)
    cast_w = w.dtype != BF16
    norm = g is not None
    row = lambda width: pl.BlockSpec((tm, width), lambda i: (i, 0))
    in_specs = [row(k) for k in ka]
    in_specs.append(pl.BlockSpec((1, kt, n), lambda i: (layer, 0, 0), pipeline_mode=pl.Buffered(1)))
    in_specs.append(row(n))
    args = list(a_list) + [w, res]
    if norm:
        in_specs.append(pl.BlockSpec((1, n), lambda i: (0, 0)))
        args.append(g.reshape(1, n))
    out_specs, out_shape = [], []
    if emit_x:
        out_specs.append(row(n))
        out_shape.append(jax.ShapeDtypeStruct((m, n), F32))
    if norm:
        out_specs.append(row(n))
        out_shape.append(jax.ShapeDtypeStruct((m, n), xn_dtype))
    outs = pl.pallas_call(
        functools.partial(_mm_full_kernel, ka=ka, cast_w=cast_w, norm=norm, emit_x=emit_x),
        grid=(m // tm,),
        in_specs=in_specs,
        out_specs=out_specs,
        out_shape=out_shape,
        scratch_shapes=[pltpu.VMEM((kt, n), BF16)] if cast_w else [],
        compiler_params=_params("arbitrary"),
        name="matmul_res_norm",
    )(*args)
    x_new = outs[0] if emit_x else None
    xn = outs[-1] if norm else None
    return x_new, xn


def _halo_init(xs_ref, buf_ref, first):
    k = buf_ref.shape[1]

    @pl.when(first)
    def _():
        xs_ref[:, HALO - k:HALO, :] = buf_ref[...]


def _conv_chunk(xs_ref, x3, w_ref, b_ref, taps, seqs, r0):
    n = x3.shape[1]
    xs_ref[seqs, HALO + r0:HALO + r0 + n, :] = x3
    y = b_ref[...] + w_ref[taps - 1:taps, :] * x3
    for k in range(taps - 1):
        lo = HALO + r0 - (taps - 1) + k
        y = y + w_ref[k:k + 1, :] * xs_ref[seqs, lo:lo + n, :]
    return y


def _halo_carry(xs_ref, rows):
    xs_ref[:, 0:HALO, :] = xs_ref[:, rows:rows + HALO, :]


def _causal_conv(xs_ref, x3, buf_ref, w_ref, b_ref, first, rows, taps, carry):
    _halo_init(xs_ref, buf_ref, first)
    y = _conv_chunk(xs_ref, x3, w_ref, b_ref, taps, slice(None), 0)
    if carry:
        _halo_carry(xs_ref, rows)
    return y


def _ffn_up_kernel(*refs, ns, rows, tps, layer):
    a_ref, wg_ref, wu_ref, cw_ref, cb_ref, buf_ref = refs[:6]
    hid_ref, st_all_ref, wgs_ref, wus_ref, gs_ref = refs[-5:]
    st_ref = _own_slice(st_all_ref, layer, None)
    i = pl.program_id(1)
    _load_weight(wg_ref, wgs_ref, i == 0)
    _load_weight(wu_ref, wus_ref, i == 0)
    _halo_init(gs_ref, buf_ref, (i % tps) == 0)
    tn = wgs_ref.shape[1]
    tm = ns * rows
    rc = _row_chunk(tm)
    for c in range(tm // rc):
        a = a_ref[c * rc:(c + 1) * rc, :]
        if ns == 1:
            seqs, r0, shape = slice(0, 1), c * rc, (1, rc, tn)
        else:
            nsc = rc // rows
            seqs, r0, shape = slice(c * nsc, (c + 1) * nsc), 0, (nsc, rows, tn)
        g3 = jnp.dot(a, wgs_ref[...], preferred_element_type=F32).reshape(shape)
        u3 = jnp.dot(a, wus_ref[...], preferred_element_type=F32).reshape(shape)
        gc = _conv_chunk(gs_ref, g3, cw_ref, cb_ref, FFN_K, seqs, r0)
        hid_ref[c * rc:(c + 1) * rc, :] = (jax.nn.gelu(gc) * u3).reshape(rc, tn).astype(hid_ref.dtype)
    st_ref[...] = gs_ref[:, HALO + rows - (FFN_K - 1):HALO + rows, :]
    if tps > 1:
        _halo_carry(gs_ref, rows)


def ffn_up(xn, w_up, layer, cw, cb, buf, st_prev, seq, tm=1024, tn=512):
    m, d = xn.shape
    nb = m // seq
    depth = w_up.shape[0]
    tm = min(tm, m)
    rows = min(seq, tm)
    ns = tm // rows
    tps = seq // rows
    nj = D_FF // tn
    kern = functools.partial(_ffn_up_kernel, ns=ns, rows=rows, tps=tps, layer=layer)
    st_blk = (ns, FFN_K - 1, tn)
    in_specs = [
        pl.BlockSpec((tm, d), lambda j, i: (i, 0)),
        pl.BlockSpec((1, d, tn), lambda j, i: (layer, 0, j)),
        pl.BlockSpec((1, d, tn), lambda j, i: (layer, 0, nj + j)),
        pl.BlockSpec((FFN_K, tn), lambda j, i: (0, j)),
        pl.BlockSpec((1, tn), lambda j, i: (0, j)),
        pl.BlockSpec((None,) + st_blk, lambda j, i: (layer, i // tps, 0, j)),
    ]
    args = [xn, w_up, w_up, cw, cb, buf]
    aliases = {}
    if layer > 0:
        in_specs.append(pl.BlockSpec(memory_space=pl.ANY))
        args.append(st_prev)
        aliases = {len(args) - 1: 1}
    return pl.pallas_call(
        kern,
        grid=(nj, m // tm),
        in_specs=in_specs,
        out_specs=[
            pl.BlockSpec((tm, tn), lambda j, i: (i, j)),
            _stacked_out_spec(depth, layer, st_blk, lambda j, i: (i // tps, 0, j)),
        ],
        out_shape=[
            jax.ShapeDtypeStruct((m, D_FF), BF16),
            jax.ShapeDtypeStruct((depth, nb, FFN_K - 1, D_FF), F32),
        ],
        scratch_shapes=[pltpu.VMEM((d, tn), BF16), pltpu.VMEM((d, tn), BF16),
                        pltpu.VMEM((ns, HALO + rows, tn), F32)],
        input_output_aliases=aliases,
        compiler_params=_params("parallel", "arbitrary"),
        name="ffn_up",
    )(*args)


def _seq_scan(a, b, h0, ns, rows):
    c = a.shape[1]
    groups = rows // SUBLANES
    assert ns == 1 or groups == 1
    a3 = a.reshape(ns * groups, SUBLANES, c)
    b3 = b.reshape(ns * groups, SUBLANES, c)
    pos = lax.broadcasted_iota(jnp.int32, a3.shape, 1)
    d = 1
    while d < SUBLANES:
        m = pos >= d
        a_sh = jnp.where(m, pltpu.roll(a3, d, 1), 1.0)
        b_sh = jnp.where(m, pltpu.roll(b3, d, 1), 0.0)
        b3 = a3 * b_sh + b3
        a3 = a3 * a_sh
        d *= 2
    if groups == 1:
        return (a3 * h0 + b3).reshape(ns * rows, c)
    h_prev = h0[0]
    hs = []
    for g in range(groups):
        hg = a3[g] * h_prev + b3[g]
        hs.append(hg)
        h_prev = hg[SUBLANES - 1:SUBLANES, :]
    return jnp.concatenate(hs, axis=0)


def _rglru_kernel(x_ref, gate_ref, cw_ref, cb_ref, wa_ref, wx_ref, ba_ref, bx_ref, lam_ref, buf_ref, h0_ref,
                  out_ref, cst_ref, hl_ref, xs_ref, *, ns, rows, tps):
    c = x_ref.shape[1]
    first = pl.program_id(2) == 0

    @pl.when(first)
    def _():
        hl_ref[...] = h0_ref[...]

    x3 = x_ref[...].reshape(ns, rows, c)
    xc = _causal_conv(xs_ref, x3, buf_ref, cw_ref, cb_ref, first, rows, CONV_K, tps > 1)
    cst_ref[...] = x3[:, rows - (CONV_K - 1):rows, :]
    xc = xc.reshape(ns * rows, c)
    xb = xc.astype(BF16)
    r = jax.nn.sigmoid(jnp.dot(xb, wa_ref[0], preferred_element_type=F32) + ba_ref[...])
    i = jax.nn.sigmoid(jnp.dot(xb, wx_ref[0], preferred_element_type=F32) + bx_ref[...])
    nl = -lam_ref[...]
    softplus = jnp.maximum(nl, 0.0) + jnp.log1p(jnp.exp(-jnp.abs(nl)))
    log_a = (-RG_C) * r * softplus
    th = jnp.tanh(log_a)
    u = jnp.sqrt(-2.0 * th / (1.0 - th)) * (i * xc)
    h = _seq_scan(jnp.exp(log_a), u, hl_ref[...], ns, rows)
    hl_ref[...] = h.reshape(ns, rows, c)[:, rows - 1:rows, :]
    out_ref[...] = (jax.nn.gelu(gate_ref[...]) * h).astype(out_ref.dtype)


def rglru(proj, cw, cb, wa_bd, wx_bd, ba, bx, lam, buf, layer, h0, seq, rows, ns):
    assert ns == 1 or rows == seq
    m = proj.shape[0]
    nb = m // seq
    tps = seq // rows
    tm = ns * rows
    c = RG_CB
    gate0 = CONV_W // c
    kern = functools.partial(_rglru_kernel, ns=ns, rows=rows, tps=tps)
    vec = pl.BlockSpec((1, c), lambda b, cb_, t: (0, cb_))
    return pl.pallas_call(
        kern,
        grid=(nb // ns, D_RG // c, tps),
        in_specs=[
            pl.BlockSpec((tm, c), lambda b, cb_, t: (b * tps + t, cb_)),
            pl.BlockSpec((tm, c), lambda b, cb_, t: (b * tps + t, gate0 + cb_)),
            pl.BlockSpec((CONV_K, c), lambda b, cb_, t: (0, cb_)),
            vec,
            pl.BlockSpec((1, c, c), lambda b, cb_, t: (cb_, 0, 0)),
            pl.BlockSpec((1, c, c), lambda b, cb_, t: (cb_, 0, 0)),
            vec, vec, vec,
            pl.BlockSpec((None, ns, CONV_K - 1, c), lambda b, cb_, t: (layer, b, 0, cb_)),
            pl.BlockSpec((ns, 1, c), lambda b, cb_, t: (b, 0, cb_)),
        ],
        out_specs=[
            pl.BlockSpec((tm, c), lambda b, cb_, t: (b * tps + t, cb_)),
            pl.BlockSpec((ns, CONV_K - 1, c), lambda b, cb_, t: (b, 0, cb_)),
            pl.BlockSpec((ns, 1, c), lambda b, cb_, t: (b, 0, cb_)),
        ],
        out_shape=[
            jax.ShapeDtypeStruct((m, D_RG), BF16),
            jax.ShapeDtypeStruct((nb, CONV_K - 1, D_RG), F32),
            jax.ShapeDtypeStruct((nb, 1, D_RG), F32),
        ],
        scratch_shapes=[pltpu.VMEM((ns, HALO + rows, c), F32)],
        compiler_params=_params("parallel", "parallel", "arbitrary"),
        name="rglru",
    )(proj, proj, cw, cb, wa_bd, wx_bd, ba, bx, lam, buf, h0)


def _heads(x3, width, ns, rows):
    parts = [x3[:, None, :, h * width:(h + 1) * width] for h in range(ML_HEADS)]
    return jnp.concatenate(parts, axis=1).reshape(ns * ML_HEADS, rows, width)


def _own_slice(all_ref, layer, first):
    if layer > 0:
        return all_ref
    n_layers = all_ref.shape[0]

    def fill():
        all_ref[1:] = jnp.zeros((n_layers - 1,) + all_ref.shape[1:], all_ref.dtype)

    if n_layers > 1:
        if first is None:
            fill()
        else:
            pl.when(first)(fill)
    return all_ref.at[0]


def _mlstm_kernel(*refs, ns, rows, tps, layer):
    qk_ref, v_ref, o_ref, xn_ref, wg_ref, cw_ref, cb_ref, gb_ref, ng_ref, buf_ref, c0_ref, n0_ref, m0_ref = refs[:13]
    out_ref, cst_ref, c_all_ref, n_ref, m_ref, xs_ref = refs[-6:]
    G = ns * ML_HEADS
    first = pl.program_id(1) == 0
    c_ref = _own_slice(c_all_ref, layer, first)

    @pl.when(first)
    def _():
        c_ref[...] = c0_ref[...]
        n_ref[...] = n0_ref[...]
        m_ref[...] = m0_ref[...]

    x3 = qk_ref[...].reshape(ns, rows, 2 * ML_HK)
    xc = _causal_conv(xs_ref, x3, buf_ref, cw_ref, cb_ref, first, rows, CONV_K, tps > 1)
    cst_ref[...] = x3[:, rows - (CONV_K - 1):rows, :]
    qk = jax.nn.silu(xc)
    q = _heads(qk[:, :, :ML_HK], ML_DK, ns, rows)
    k = _heads(qk[:, :, ML_HK:], ML_DK, ns, rows) * (ML_DK ** -0.5)
    v = _heads(v_ref[...].reshape(ns, rows, ML_HV), ML_DV, ns, rows)
    gates = jnp.dot(xn_ref[...], wg_ref[0], preferred_element_type=F32).reshape(ns, rows, LANES) + gb_ref[...]
    i_col = _heads(gates[:, :, :ML_HEADS], 1, ns, rows)
    f_col = _heads(jax.nn.log_sigmoid(gates[:, :, ML_HEADS:2 * ML_HEADS]), 1, ns, rows)

    t_idx = lax.broadcasted_iota(jnp.int32, (rows, rows), 0)
    s_idx = lax.broadcasted_iota(jnp.int32, (rows, rows), 1)
    eye = t_idx == s_idx
    causal = t_idx >= s_idx
    i_row = jnp.sum(jnp.where(eye, i_col, 0.0), axis=1, keepdims=True)
    f_row = jnp.sum(jnp.where(eye, f_col, 0.0), axis=1, keepdims=True)
    b_col = jnp.sum(jnp.where(causal, f_row, 0.0), axis=2, keepdims=True)
    b_row = jnp.sum(jnp.where(t_idx <= s_idx, f_col, 0.0), axis=1, keepdims=True)
    dmat = jnp.where(causal, b_col - b_row + i_row, -jnp.inf)

    m_prev = m_ref[...].reshape(G, 1, LANES)[:, :, 0:1]
    inter = b_col + m_prev
    m_t = jnp.maximum(inter, jnp.max(dmat, axis=2, keepdims=True))
    qb, kb, vb = q.astype(BF16), k.astype(BF16), v.astype(BF16)
    s = jnp.einsum("gld,gsd->gls", qb, kb, preferred_element_type=F32) * jnp.exp(dmat - m_t)
    w_inter = jnp.exp(inter - m_t)
    c_old = c_ref[...].reshape(G, ML_DK, ML_DV)
    n_old = n_ref[...].reshape(G, 1, ML_DK)
    num = (w_inter * jnp.einsum("gld,gde->gle", qb, c_old.astype(BF16), preferred_element_type=F32)
           + jnp.einsum("gls,gse->gle", s.astype(BF16), vb, preferred_element_type=F32))
    den = w_inter * jnp.sum(q * n_old, axis=2, keepdims=True) + jnp.sum(s, axis=2, keepdims=True)
    h = num / jnp.maximum(jnp.abs(den), jnp.exp(-m_t))

    m_new = m_t[:, rows - 1:rows, :]
    b_last = b_col[:, rows - 1:rows, :]
    w_state = jnp.exp(b_last - b_col + i_col - m_new)
    decay = jnp.exp(b_last + m_prev - m_new)
    kw = k * w_state
    kwt = jnp.swapaxes(kw, 1, 2).astype(BF16)
    c_new = decay * c_old + jnp.einsum("gds,gse->gde", kwt, vb, preferred_element_type=F32)
    n_new = decay * n_old + jnp.sum(kw, axis=1, keepdims=True)
    c_ref[...] = c_new.reshape(ns, ML_HEADS, ML_DK, ML_DV)
    n_ref[...] = n_new.reshape(ns, ML_HEADS, 1, ML_DK)
    m_ref[...] = jnp.broadcast_to(m_new, (G, 1, LANES)).reshape(ns, ML_HEADS, 1, LANES)

    hn = h * lax.rsqrt(jnp.mean(h * h, axis=2, keepdims=True) + EPS)
    hn = hn.reshape(ns, ML_HEADS, rows, ML_DV)
    o3 = o_ref[...].reshape(ns, rows, ML_HV)
    for hh in range(ML_HEADS):
        sl = slice(hh * ML_DV, (hh + 1) * ML_DV)
        y = hn[:, hh] * ng_ref[:, sl] * jax.nn.sigmoid(o3[:, :, sl])
        out_ref[:, sl] = y.reshape(ns * rows, ML_DV).astype(out_ref.dtype)


def _stacked_out_spec(n_layers, layer, blk, idx):
    if layer == 0:
        return pl.BlockSpec((n_layers,) + blk, lambda *g: (0,) + idx(*g))
    return pl.BlockSpec((None,) + blk, lambda *g: (layer,) + idx(*g))


def mlstm(proj, xn, w_gate, cw, cb, gbias, ng, buf, c0, layer, n0, m0, c_prev, seq, rows, ns):
    assert ns == 1 or rows == seq
    m = proj.shape[0]
    nb = m // seq
    n_layers = c0.shape[0]
    tps = seq // rows
    tm = ns * rows
    w = 2 * ML_HK
    kern = functools.partial(_mlstm_kernel, ns=ns, rows=rows, tps=tps, layer=layer)
    row_blk = lambda col: pl.BlockSpec((tm, w), lambda b, t: (b * tps + t, col))
    st4 = lambda *shape: pl.BlockSpec((ns,) + shape, lambda b, t: (b, 0, 0, 0))
    c_blk = (ns, ML_HEADS, ML_DK, ML_DV)
    in_specs = [
        row_blk(D_RG // w),
        row_blk((CONV_W + D_RG) // w),
        row_blk((CONV_W + D_RG + ML_HV) // w),
        pl.BlockSpec((tm, D_MODEL), lambda b, t: (b * tps + t, 0)),
        pl.BlockSpec((1, D_MODEL, LANES), lambda b, t: (layer, 0, 0)),
        pl.BlockSpec((CONV_K, w), lambda b, t: (0, D_RG // w)),
        pl.BlockSpec((1, w), lambda b, t: (0, D_RG // w)),
        pl.BlockSpec((1, LANES), lambda b, t: (0, 0)),
        pl.BlockSpec((1, ML_HV), lambda b, t: (0, 0)),
        pl.BlockSpec((None, ns, CONV_K - 1, w), lambda b, t: (layer, b, 0, D_RG // w)),
        pl.BlockSpec((None,) + c_blk, lambda b, t: (layer, b, 0, 0, 0)),
        st4(ML_HEADS, 1, ML_DK),
        st4(ML_HEADS, 1, LANES),
    ]
    args = [proj, proj, proj, xn, w_gate, cw, cb, gbias, ng, buf, c0, n0, m0]
    aliases = {}
    if layer > 0:
        in_specs.append(pl.BlockSpec(memory_space=pl.ANY))
        args.append(c_prev)
        aliases = {len(args) - 1: 2}
    return pl.pallas_call(
        kern,
        grid=(nb // ns, tps),
        in_specs=in_specs,
        out_specs=[
            pl.BlockSpec((tm, ML_HV), lambda b, t: (b * tps + t, 0)),
            pl.BlockSpec((ns, CONV_K - 1, w), lambda b, t: (b, 0, 0)),
            _stacked_out_spec(n_layers, layer, c_blk, lambda b, t: (b, 0, 0, 0)),
            st4(ML_HEADS, 1, ML_DK),
            st4(ML_HEADS, 1, LANES),
        ],
        out_shape=[
            jax.ShapeDtypeStruct((m, ML_HV), BF16),
            jax.ShapeDtypeStruct((nb, CONV_K - 1, w), F32),
            jax.ShapeDtypeStruct((n_layers, nb) + c_blk[1:], F32),
            jax.ShapeDtypeStruct((nb, ML_HEADS, 1, ML_DK), F32),
            jax.ShapeDtypeStruct((nb, ML_HEADS, 1, LANES), F32),
        ],
        scratch_shapes=[pltpu.VMEM((ns, HALO + rows, w), F32)],
        input_output_aliases=aliases,
        compiler_params=_params("parallel", "arbitrary"),
        name="mlstm",
    )(*args)


def _layernorm(v, g_ref, b_ref):
    mu = jnp.mean(v, axis=-1, keepdims=True)
    vc = v - mu
    var = jnp.mean(vc * vc, axis=-1, keepdims=True)
    return vc * lax.rsqrt(var + EPS) * g_ref[...] + b_ref[...]


def _odd_in_proj(a, w_ref):
    u = jax.nn.gelu(jnp.dot(a, w_ref[0, :, 0:D_C], preferred_element_type=F32))
    v = jax.nn.gelu(jnp.dot(a, w_ref[0, :, D_C:2 * D_C], preferred_element_type=F32))
    return u, v


def _odd_long_kernel(a_ref, w_ref, lg_ref, lb_ref, ws_ref, sb_ref, out_ref):
    n = a_ref.shape[0] // C_CHUNK
    nxt = _odd_in_proj(a_ref[0:C_CHUNK, :], w_ref)
    for c in range(n):
        rs = slice(c * C_CHUNK, (c + 1) * C_CHUNK)
        u, v = nxt
        if c + 1 < n:
            nxt = _odd_in_proj(a_ref[(c + 1) * C_CHUNK:(c + 2) * C_CHUNK, :], w_ref)
        vn = _layernorm(v, lg_ref, lb_ref).astype(BF16)
        for g in range(C_GROUPS):
            sl = slice(g * C_GW, (g + 1) * C_GW)
            mix = jnp.dot(ws_ref[0, g], vn[:, sl], preferred_element_type=F32) + sb_ref[:, g:g + 1]
            out_ref[rs, sl] = (u[:, sl] * mix).astype(out_ref.dtype)


def _odd_short_kernel(a_ref, w_ref, lg_ref, lb_ref, we_ref, be_ref, out_ref, vn_ref, *, rows):
    rc = C_CHUNK
    nsc = rc // rows
    for c in range(a_ref.shape[0] // rc):
        rs = slice(c * rc, (c + 1) * rc)
        u, v = _odd_in_proj(a_ref[rs, :], w_ref)
        vn = _layernorm(v, lg_ref, lb_ref)
        vn_ref[rs, :] = vn
        v3 = vn.reshape(nsc, rows, D_C)
        mix = jnp.broadcast_to(be_ref[...], (nsc, rows, D_C))
        for s in range(rows):
            mix = mix + we_ref[s] * v3[:, s:s + 1, :]
        out_ref[rs, :] = (u * mix.reshape(rc, D_C)).astype(out_ref.dtype)


def odd_mixer(xn, w, layer, lg, lb, mix_w, mix_b, rows=None, tm=512):
    m, d = xn.shape
    tm = min(tm, m)
    row = lambda width: pl.BlockSpec((tm, width), lambda i: (i, 0))
    vec = pl.BlockSpec((1, D_C), lambda i: (0, 0))
    in_specs = [row(d), pl.BlockSpec((1, d, 2 * D_C), lambda i: (layer, 0, 0), pipeline_mode=pl.Buffered(1)), vec, vec]
    if rows is None:
        kern = _odd_long_kernel
        in_specs += [pl.BlockSpec((1, C_GROUPS, C_CHUNK, C_CHUNK), lambda i: (layer, 0, 0, 0)),
                     pl.BlockSpec((C_CHUNK, C_GROUPS), lambda i: (0, 0))]
        out_specs = [row(D_C)]
        out_shape = [jax.ShapeDtypeStruct((m, D_C), BF16)]
    else:
        kern = functools.partial(_odd_short_kernel, rows=rows)
        in_specs += [pl.BlockSpec((rows, rows, D_C), lambda i: (0, 0, 0)), pl.BlockSpec((rows, D_C), lambda i: (0, 0))]
        out_specs = [row(D_C), row(D_C)]
        out_shape = [jax.ShapeDtypeStruct((m, D_C), BF16), jax.ShapeDtypeStruct((m, D_C), F32)]
    return pl.pallas_call(
        kern,
        grid=(m // tm,),
        in_specs=in_specs,
        out_specs=out_specs,
        out_shape=out_shape,
        compiler_params=_params("arbitrary"),
        name="odd_mixer",
    )(xn, w, lg, lb, mix_w, mix_b)


def _block_diag(w):
    per = RG_CB // RG_BW
    w = w.reshape(D_RG // RG_CB, per, RG_BW, RG_BW)
    eye = jnp.eye(per, dtype=w.dtype)
    bd = w[:, :, :, None, :] * eye[None, :, None, :, None]
    return bd.reshape(D_RG // RG_CB, RG_CB, RG_CB).astype(BF16)


def _prep_weights(P):
    n_even = P["w_in_even"].shape[0]
    pad = LANES - 2 * ML_HEADS
    return {
        "in_even": P["w_in_even"].astype(BF16),
        "in_gate": jnp.pad(P["w_in_even"][:, :, E_MAIN:], ((0, 0), (0, 0), (0, pad))).astype(BF16),
        "wa": [_block_diag(P["rg_wa"][j]) for j in range(n_even)],
        "wx": [_block_diag(P["rg_wx"][j]) for j in range(n_even)],
        "gbias": [jnp.pad(P["ml_gate_b"][j].reshape(1, 2 * ML_HEADS), ((0, 0), (0, pad))) for j in range(n_even)],
        "in_odd": P["w_in_odd"].astype(BF16),
        "sgu_ws": jnp.tril(P["sgu_ws"]).astype(BF16),
        "down": P["ffn_w_down"].astype(BF16),
    }


def _sgu_short_weights(ws, sb, rows):
    w = jnp.tril(ws)[:, :rows, :rows]
    w_exp = jnp.repeat(jnp.transpose(w, (2, 1, 0)), C_GW, axis=2)
    b_exp = jnp.repeat(jnp.transpose(sb[:, :rows]), C_GW, axis=1)
    return w_exp, b_exp


def _trunk(x, conv_buf, rg_h, ml_c, ml_n, ml_m, ffn_buf, P, W, short):
    nb, seq, d = x.shape
    m = nb * seq
    depth = P["ffn_w_up"].shape[0]
    x = x.reshape(m, d)
    if short:
        rg_rows, rg_ns = seq, 32
        ml_rows, ml_ns = seq, 8
    else:
        rg_rows, rg_ns = 256, 1
        ml_rows, ml_ns = 128, 1
    conv_l, h_l, n_l, m_l, v_l = [], [], [], [], []
    c_all = f_all = None
    xn = rmsnorm(x, P["norm_mix"][0], BF16)
    for layer in range(depth):
        j = layer // 2
        g_ffn = P["norm_ffn"][layer]
        if layer % 2 == 0:
            proj = matmul_nt(xn, W["in_even"], j, E_MAIN)
            cw, cb = P["conv_even_w"][j], P["conv_even_b"][j].reshape(1, CONV_W)
            rg_out, cst_rg, h_last = rglru(
                proj, cw, cb, W["wa"][j], W["wx"][j], P["rg_ba"][j].reshape(1, D_RG), P["rg_bx"][j].reshape(1, D_RG),
                P["rg_lambda"][j].reshape(1, D_RG), conv_buf, j, rg_h[j].reshape(nb, 1, D_RG), seq, rg_rows, rg_ns)
            ml_out, cst_ml, c_all, n_new, m_new = mlstm(
                proj, xn, W["in_gate"], cw, cb, W["gbias"][j], P["ml_norm_g"][j].reshape(1, ML_HV), conv_buf, ml_c, j,
                ml_n[j].reshape(nb, ML_HEADS, 1, ML_DK),
                jnp.broadcast_to(ml_m[j][:, :, None, None], (nb, ML_HEADS, 1, LANES)), c_all, seq, ml_rows, ml_ns)
            x, xn = matmul_res_norm([rg_out, ml_out], P["w_out_even"], j, x, g=g_ffn)
            conv_l.append(jnp.concatenate([cst_rg, cst_ml], axis=-1))
            h_l.append(h_last.reshape(nb, D_RG))
            n_l.append(n_new.reshape(nb, ML_HEADS, ML_DK))
            m_l.append(m_new[:, :, 0, 0])
        else:
            lg, lb = P["sgu_ln_g"][j].reshape(1, D_C), P["sgu_ln_b"][j].reshape(1, D_C)
            if short:
                w_exp, b_exp = _sgu_short_weights(P["sgu_ws"][j], P["sgu_b"][j], seq)
                gated, vn = odd_mixer(xn, W["in_odd"], j, lg, lb, w_exp, b_exp, rows=seq)
                v_l.append(vn.reshape(nb, seq, D_C))
            else:
                gated, = odd_mixer(xn, W["in_odd"], j, lg, lb, W["sgu_ws"], jnp.transpose(P["sgu_b"][j]))
            x, xn = matmul_res_norm([gated], P["w_out_odd"], j, x, g=g_ffn)
        hid, f_all = ffn_up(xn, P["ffn_w_up"], layer, P["ffn_conv_w"][layer], P["ffn_conv_b"][layer].reshape(1, D_FF),
                            ffn_buf, f_all, seq)
        last = layer == depth - 1
        g_next = P["norm_final"] if last else P["norm_mix"][layer + 1]
        x, xn = matmul_res_norm([hid], W["down"], layer, x, g=g_next, xn_dtype=F32 if last else BF16,
                                emit_x=not last, tm=256)
    y = xn.reshape(nb, seq, d)
    return y, jnp.stack(conv_l), jnp.stack(h_l), c_all, jnp.stack(n_l), jnp.stack(m_l), v_l, f_all


def kernel(x_prompt, x_sample, state_conv_mix, state_rglru_h, state_mlstm_C, state_mlstm_n, state_mlstm_m, state_ffn_conv, norm_mix, norm_ffn, norm_final, w_in_even, conv_even_w, conv_even_b, rg_wa, rg_ba, rg_wx, rg_bx, rg_lambda, ml_gate_b, ml_norm_g, w_out_even, w_in_odd, sgu_ln_g, sgu_ln_b, sgu_ws, sgu_b, w_out_odd, ffn_w_up, ffn_conv_w, ffn_conv_b, ffn_w_down):
    P = {
        "norm_mix": norm_mix, "norm_ffn": norm_ffn, "norm_final": norm_final,
        "w_in_even": w_in_even, "conv_even_w": conv_even_w, "conv_even_b": conv_even_b,
        "rg_wa": rg_wa, "rg_ba": rg_ba, "rg_wx": rg_wx, "rg_bx": rg_bx, "rg_lambda": rg_lambda,
        "ml_gate_b": ml_gate_b, "ml_norm_g": ml_norm_g, "w_out_even": w_out_even,
        "w_in_odd": w_in_odd, "sgu_ln_g": sgu_ln_g, "sgu_ln_b": sgu_ln_b, "sgu_ws": sgu_ws,
        "sgu_b": sgu_b, "w_out_odd": w_out_odd, "ffn_w_up": ffn_w_up, "ffn_conv_w": ffn_conv_w,
        "ffn_conv_b": ffn_conv_b, "ffn_w_down": ffn_w_down,
    }
    W = _prep_weights(P)
    n_even, depth = w_in_even.shape[0], ffn_w_up.shape[0]
    bp = x_prompt.shape[0]
    dt = x_prompt.dtype
    z_conv = jnp.zeros((n_even, bp, CONV_K - 1, CONV_W), dt)
    z_h = jnp.zeros((n_even, bp, D_RG), dt)
    z_c = jnp.zeros((n_even, bp, ML_HEADS, ML_DK, ML_DV), dt)
    z_n = jnp.zeros((n_even, bp, ML_HEADS, ML_DK), dt)
    z_m = jnp.zeros((n_even, bp, ML_HEADS), dt)
    z_f = jnp.zeros((depth, bp, FFN_K - 1, D_FF), dt)
    y_p, conv_p, h_p, c_p, n_p, m_p, _, f_p = _trunk(x_prompt, z_conv, z_h, z_c, z_n, z_m, z_f, P, W, short=False)
    y_s, conv_s, h_s, c_s, n_s, m_s, v_list, f_s = _trunk(
        x_sample, state_conv_mix, state_rglru_h, state_mlstm_C, state_mlstm_n, state_mlstm_m, state_ffn_conv,
        P, W, short=True)
    chunk_v_s = jnp.stack(v_list)
    return (y_p, y_s, conv_p, conv_s, h_p, h_s, c_p, c_s, n_p, n_s, m_p, m_s, chunk_v_s, f_p, f_s)
```

```python
import functools

import jax
import jax.numpy as jnp
from jax import lax
from jax.experimental import pallas as pl
from jax.experimental.pallas import tpu as pltpu

F32 = jnp.float32
BF16 = jnp.bfloat16

D_MODEL = 2048
EPS = 1e-6
D_RG = D_MODEL // 2
RG_BLOCKS = 16
RG_BW = D_RG // RG_BLOCKS
RG_C = 8.0
CONV_K = 4
ML_HEADS = 4
ML_DK = D_MODEL // 16
ML_DV = D_MODEL // 8
ML_HK = ML_HEADS * ML_DK
ML_HV = ML_HEADS * ML_DV
CONV_W = D_RG + 2 * ML_HK
E_MAIN = CONV_W + D_RG + 2 * ML_HV
D_C = D_MODEL
C_GROUPS = 8
C_GW = D_C // C_GROUPS
C_CHUNK = 128
D_FF = 5632
FFN_K = 3

SUBLANES = 8
LANES = 128
HALO = SUBLANES
RG_CB = 256
VMEM_LIMIT = 56 * 1024 * 1024
ROW_CHUNK = 256


def _params(*sem):
    return pltpu.CompilerParams(dimension_semantics=sem, vmem_limit_bytes=VMEM_LIMIT)


def _rmsnorm_kernel(x_ref, g_ref, o_ref):
    x = x_ref[...]
    ms = jnp.mean(x * x, axis=-1, keepdims=True)
    o_ref[...] = (x * lax.rsqrt(ms + EPS) * g_ref[...]).astype(o_ref.dtype)


def rmsnorm(x, g, out_dtype, tm=512):
    m, d = x.shape
    return pl.pallas_call(
        _rmsnorm_kernel,
        grid=(m // tm,),
        in_specs=[pl.BlockSpec((tm, d), lambda i: (i, 0)), pl.BlockSpec((1, d), lambda i: (0, 0))],
        out_specs=pl.BlockSpec((tm, d), lambda i: (i, 0)),
        out_shape=jax.ShapeDtypeStruct((m, d), out_dtype),
        compiler_params=_params("parallel"),
        name="rmsnorm",
    )(x, g.reshape(1, d))


def _row_chunk(tm):
    return min(tm, ROW_CHUNK)


def _load_weight(w_ref, ws_ref, first):
    @pl.when(first)
    def _():
        ws_ref[...] = w_ref[0].astype(BF16)


def _mm_nt_kernel(a_ref, w_ref, o_ref):
    tm = a_ref.shape[0]
    rc = _row_chunk(tm)
    for c in range(tm // rc):
        rs = slice(c * rc, (c + 1) * rc)
        o_ref[rs, :] = jnp.dot(a_ref[rs, :], w_ref[0], preferred_element_type=F32).astype(o_ref.dtype)


def matmul_nt(a, w, layer, n_out, tm=1024, tn=1024):
    m, k = a.shape
    tm, tn = min(tm, m), min(tn, n_out)
    return pl.pallas_call(
        _mm_nt_kernel,
        grid=(n_out // tn, m // tm),
        in_specs=[
            pl.BlockSpec((tm, k), lambda j, i: (i, 0)),
            pl.BlockSpec((1, k, tn), lambda j, i: (layer, 0, j)),
        ],
        out_specs=pl.BlockSpec((tm, tn), lambda j, i: (i, j)),
        out_shape=jax.ShapeDtypeStruct((m, n_out), F32),
        compiler_params=_params("parallel", "arbitrary"),
        name="matmul_nt",
    )(a, w)


def _mm_full_kernel(*refs, ka, cast_w, norm, emit_x):
    n_a = len(ka)
    a_refs, w_ref, res_ref = refs[:n_a], refs[n_a], refs[n_a + 1]
    pos = n_a + 2
    g_ref = refs[pos] if norm else None
    pos += int(norm)
    x_ref = refs[pos] if emit_x else None
    pos += int(emit_x)
    xn_ref = refs[pos] if norm else None
    if cast_w:
        ws_ref = refs[-1]
        _load_weight(w_ref, ws_ref, pl.program_id(0) == 0)
        wget = lambda lo, hi: ws_ref[lo:hi, :]
    else:
        wget = lambda lo, hi: w_ref[0, lo:hi, :]
    tm = res_ref.shape[0]
    rc = _row_chunk(tm)
    for c in range(tm // rc):
        rs = slice(c * rc, (c + 1) * rc)
        y = res_ref[rs, :]
        off = 0
        for a_ref, k in zip(a_refs, ka):
            y = y + jnp.dot(a_ref[rs, :], wget(off, off + k), preferred_element_type=F32)
            off += k
        if emit_x:
            x_ref[rs, :] = y
        if norm:
            ms = jnp.mean(y * y, axis=-1, keepdims=True)
            xn_ref[rs, :] = (y * lax.rsqrt(ms + EPS) * g_ref[...]).astype(xn_ref.dtype)


def matmul_res_norm(a_list, w, layer, res, g=None, xn_dtype=BF16, emit_x=True, tm=512):
    m, n = res.shape
    ka = tuple(a.shape[1] for a in a_list)
    kt = sum(ka)
    cast_w = w.dtype != BF16
    norm = g is not None
    row = lambda width: pl.BlockSpec((tm, width), lambda i: (i, 0))
    in_specs = [row(k) for k in ka]
    in_specs.append(pl.BlockSpec((1, kt, n), lambda i: (layer, 0, 0), pipeline_mode=pl.Buffered(1)))
    in_specs.append(row(n))
    args = list(a_list) + [w, res]
    if norm:
        in_specs.append(pl.BlockSpec((1, n), lambda i: (0, 0)))
        args.append(g.reshape(1, n))
    out_specs, out_shape = [], []
    if emit_x:
        out_specs.append(row(n))
        out_shape.append(jax.ShapeDtypeStruct((m, n), F32))
    if norm:
        out_specs.append(row(n))
        out_shape.append(jax.ShapeDtypeStruct((m, n), xn_dtype))
    outs = pl.pallas_call(
        functools.partial(_mm_full_kernel, ka=ka, cast_w=cast_w, norm=norm, emit_x=emit_x),
        grid=(m // tm,),
        in_specs=in_specs,
        out_specs=out_specs,
        out_shape=out_shape,
        scratch_shapes=[pltpu.VMEM((kt, n), BF16)] if cast_w else [],
        compiler_params=_params("arbitrary"),
        name="matmul_res_norm",
    )(*args)
    x_new = outs[0] if emit_x else None
    xn = outs[-1] if norm else None
    return x_new, xn


def _halo_init(xs_ref, buf_ref, first):
    k = buf_ref.shape[1]

    @pl.when(first)
    def _():
        xs_ref[:, HALO - k:HALO, :] = buf_ref[...]


def _conv_chunk(xs_ref, x3, w_ref, b_ref, taps, seqs, r0):
    n = x3.shape[1]
    xs_ref[seqs, HALO + r0:HALO + r0 + n, :] = x3
    y = b_ref[...] + w_ref[taps - 1:taps, :] * x3
    for k in range(taps - 1):
        lo = HALO + r0 - (taps - 1) + k
        y = y + w_ref[k:k + 1, :] * xs_ref[seqs, lo:lo + n, :]
    return y


def _halo_carry(xs_ref, rows):
    xs_ref[:, 0:HALO, :] = xs_ref[:, rows:rows + HALO, :]


def _causal_conv(xs_ref, x3, buf_ref, w_ref, b_ref, first, rows, taps, carry):
    _halo_init(xs_ref, buf_ref, first)
    y = _conv_chunk(xs_ref, x3, w_ref, b_ref, taps, slice(None), 0)
    if carry:
        _halo_carry(xs_ref, rows)
    return y


def _ffn_up_kernel(*refs, ns, rows, tps, layer):
    a_ref, wg_ref, wu_ref, cw_ref, cb_ref, buf_ref = refs[:6]
    hid_ref, st_all_ref, wgs_ref, wus_ref, gs_ref = refs[-5:]
    st_ref = _own_slice(st_all_ref, layer, None)
    i = pl.program_id(1)
    _load_weight(wg_ref, wgs_ref, i == 0)
    _load_weight(wu_ref, wus_ref, i == 0)
    _halo_init(gs_ref, buf_ref, (i % tps) == 0)
    tn = wgs_ref.shape[1]
    tm = ns * rows
    rc = _row_chunk(tm)
    for c in range(tm // rc):
        a = a_ref[c * rc:(c + 1) * rc, :]
        if ns == 1:
            seqs, r0, shape = slice(0, 1), c * rc, (1, rc, tn)
        else:
            nsc = rc // rows
            seqs, r0, shape = slice(c * nsc, (c + 1) * nsc), 0, (nsc, rows, tn)
        g3 = jnp.dot(a, wgs_ref[...], preferred_element_type=F32).reshape(shape)
        u3 = jnp.dot(a, wus_ref[...], preferred_element_type=F32).reshape(shape)
        gc = _conv_chunk(gs_ref, g3, cw_ref, cb_ref, FFN_K, seqs, r0)
        hid_ref[c * rc:(c + 1) * rc, :] = (jax.nn.gelu(gc) * u3).reshape(rc, tn).astype(hid_ref.dtype)
    st_ref[...] = gs_ref[:, HALO + rows - (FFN_K - 1):HALO + rows, :]
    if tps > 1:
        _halo_carry(gs_ref, rows)


def ffn_up(xn, w_up, layer, cw, cb, buf, st_prev, seq, tm=1024, tn=512):
    m, d = xn.shape
    nb = m // seq
    depth = w_up.shape[0]
    tm = min(tm, m)
    rows = min(seq, tm)
    ns = tm // rows
    tps = seq // rows
    nj = D_FF // tn
    kern = functools.partial(_ffn_up_kernel, ns=ns, rows=rows, tps=tps, layer=layer)
    st_blk = (ns, FFN_K - 1, tn)
    in_specs = [
        pl.BlockSpec((tm, d), lambda j, i: (i, 0)),
        pl.BlockSpec((1, d, tn), lambda j, i: (layer, 0, j)),
        pl.BlockSpec((1, d, tn), lambda j, i: (layer, 0, nj + j)),
        pl.BlockSpec((FFN_K, tn), lambda j, i: (0, j)),
        pl.BlockSpec((1, tn), lambda j, i: (0, j)),
        pl.BlockSpec((None,) + st_blk, lambda j, i: (layer, i // tps, 0, j)),
    ]
    args = [xn, w_up, w_up, cw, cb, buf]
    aliases = {}
    if layer > 0:
        in_specs.append(pl.BlockSpec(memory_space=pl.ANY))
        args.append(st_prev)
        aliases = {len(args) - 1: 1}
    return pl.pallas_call(
        kern,
        grid=(nj, m // tm),
        in_specs=in_specs,
        out_specs=[
            pl.BlockSpec((tm, tn), lambda j, i: (i, j)),
            _stacked_out_spec(depth, layer, st_blk, lambda j, i: (i // tps, 0, j)),
        ],
        out_shape=[
            jax.ShapeDtypeStruct((m, D_FF), BF16),
            jax.ShapeDtypeStruct((depth, nb, FFN_K - 1, D_FF), F32),
        ],
        scratch_shapes=[pltpu.VMEM((d, tn), BF16), pltpu.VMEM((d, tn), BF16),
                        pltpu.VMEM((ns, HALO + rows, tn), F32)],
        input_output_aliases=aliases,
        compiler_params=_params("parallel", "arbitrary"),
        name="ffn_up",
    )(*args)


def _seq_scan(a, b, h0, ns, rows):
    c = a.shape[1]
    groups = rows // SUBLANES
    assert ns == 1 or groups == 1
    a3 = a.reshape(ns * groups, SUBLANES, c)
    b3 = b.reshape(ns * groups, SUBLANES, c)
    pos = lax.broadcasted_iota(jnp.int32, a3.shape, 1)
    d = 1
    while d < SUBLANES:
        m = pos >= d
        a_sh = jnp.where(m, pltpu.roll(a3, d, 1), 1.0)
        b_sh = jnp.where(m, pltpu.roll(b3, d, 1), 0.0)
        b3 = a3 * b_sh + b3
        a3 = a3 * a_sh
        d *= 2
    if groups == 1:
        return (a3 * h0 + b3).reshape(ns * rows, c)
    h_prev = h0[0]
    hs = []
    for g in range(groups):
        hg = a3[g] * h_prev + b3[g]
        hs.append(hg)
        h_prev = hg[SUBLANES - 1:SUBLANES, :]
    return jnp.concatenate(hs, axis=0)


def _rglru_kernel(x_ref, gate_ref, cw_ref, cb_ref, wa_ref, wx_ref, ba_ref, bx_ref, lam_ref, buf_ref, h0_ref,
                  out_ref, cst_ref, hl_ref, xs_ref, *, ns, rows, tps):
    first = pl.program_id(1) == 0

    @pl.when(first)
    def _():
        hl_ref[...] = h0_ref[...]

    x3 = x_ref[...].reshape(ns, rows, D_RG)
    xc_all = _causal_conv(xs_ref, x3, buf_ref, cw_ref, cb_ref, first, rows, CONV_K, tps > 1)
    cst_ref[...] = x3[:, rows - (CONV_K - 1):rows, :]
    xc_all = xc_all.reshape(ns * rows, D_RG)
    nl = -lam_ref[...]
    softplus = jnp.maximum(nl, 0.0) + jnp.log1p(jnp.exp(-jnp.abs(nl)))
    for blk in range(D_RG // RG_CB):
        sl = slice(blk * RG_CB, (blk + 1) * RG_CB)
        xc = xc_all[:, sl]
        xb = xc.astype(BF16)
        r = jax.nn.sigmoid(jnp.dot(xb, wa_ref[blk], preferred_element_type=F32) + ba_ref[:, sl])
        i = jax.nn.sigmoid(jnp.dot(xb, wx_ref[blk], preferred_element_type=F32) + bx_ref[:, sl])
        log_a = (-RG_C) * r * softplus[:, sl]
        th = jnp.tanh(log_a)
        u = jnp.sqrt(-2.0 * th / (1.0 - th)) * (i * xc)
        h = _seq_scan(jnp.exp(log_a), u, hl_ref[:, :, sl], ns, rows)
        hl_ref[:, :, sl] = h.reshape(ns, rows, RG_CB)[:, rows - 1:rows, :]
        out_ref[:, sl] = (jax.nn.gelu(gate_ref[:, sl]) * h).astype(out_ref.dtype)


def rglru(proj, cw, cb, wa_bd, wx_bd, ba, bx, lam, buf, layer, h0, seq, rows, ns):
    assert ns == 1 or rows == seq
    m = proj.shape[0]
    nb = m // seq
    tps = seq // rows
    tm = ns * rows
    c = D_RG
    nblk = D_RG // RG_CB
    kern = functools.partial(_rglru_kernel, ns=ns, rows=rows, tps=tps)
    vec = pl.BlockSpec((1, c), lambda b, t: (0, 0))
    return pl.pallas_call(
        kern,
        grid=(nb // ns, tps),
        in_specs=[
            pl.BlockSpec((tm, c), lambda b, t: (b * tps + t, 0)),
            pl.BlockSpec((tm, c), lambda b, t: (b * tps + t, CONV_W // c)),
            pl.BlockSpec((CONV_K, c), lambda b, t: (0, 0)),
            vec,
            pl.BlockSpec((nblk, RG_CB, RG_CB), lambda b, t: (0, 0, 0)),
            pl.BlockSpec((nblk, RG_CB, RG_CB), lambda b, t: (0, 0, 0)),
            vec, vec, vec,
            pl.BlockSpec((None, ns, CONV_K - 1, c), lambda b, t: (layer, b, 0, 0)),
            pl.BlockSpec((ns, 1, c), lambda b, t: (b, 0, 0)),
        ],
        out_specs=[
            pl.BlockSpec((tm, c), lambda b, t: (b * tps + t, 0)),
            pl.BlockSpec((ns, CONV_K - 1, c), lambda b, t: (b, 0, 0)),
            pl.BlockSpec((ns, 1, c), lambda b, t: (b, 0, 0)),
        ],
        out_shape=[
            jax.ShapeDtypeStruct((m, D_RG), BF16),
            jax.ShapeDtypeStruct((nb, CONV_K - 1, D_RG), F32),
            jax.ShapeDtypeStruct((nb, 1, D_RG), F32),
        ],
        scratch_shapes=[pltpu.VMEM((ns, HALO + rows, c), F32)],
        compiler_params=_params("parallel", "arbitrary"),
        name="rglru",
    )(proj, proj, cw, cb, wa_bd, wx_bd, ba, bx, lam, buf, h0)


def _heads(x3, width, ns, rows):
    parts = [x3[:, None, :, h * width:(h + 1) * width] for h in range(ML_HEADS)]
    return jnp.concatenate(parts, axis=1).reshape(ns * ML_HEADS, rows, width)


def _own_slice(all_ref, layer, first):
    if layer > 0:
        return all_ref
    n_layers = all_ref.shape[0]

    def fill():
        all_ref[1:] = jnp.zeros((n_layers - 1,) + all_ref.shape[1:], all_ref.dtype)

    if n_layers > 1:
        if first is None:
            fill()
        else:
            pl.when(first)(fill)
    return all_ref.at[0]


def _mlstm_kernel(*refs, ns, rows, tps, layer):
    qk_ref, v_ref, o_ref, xn_ref, wg_ref, cw_ref, cb_ref, gb_ref, ng_ref, buf_ref, c0_ref, n0_ref, m0_ref = refs[:13]
    out_ref, cst_ref, c_all_ref, n_ref, m_ref, xs_ref = refs[-6:]
    G = ns * ML_HEADS
    first = pl.program_id(1) == 0
    c_ref = _own_slice(c_all_ref, layer, first)

    @pl.when(first)
    def _():
        c_ref[...] = c0_ref[...]
        n_ref[...] = n0_ref[...]
        m_ref[...] = m0_ref[...]

    x3 = qk_ref[...].reshape(ns, rows, 2 * ML_HK)
    xc = _causal_conv(xs_ref, x3, buf_ref, cw_ref, cb_ref, first, rows, CONV_K, tps > 1)
    cst_ref[...] = x3[:, rows - (CONV_K - 1):rows, :]
    qk = jax.nn.silu(xc)
    q = _heads(qk[:, :, :ML_HK], ML_DK, ns, rows)
    k = _heads(qk[:, :, ML_HK:], ML_DK, ns, rows) * (ML_DK ** -0.5)
    v = _heads(v_ref[...].reshape(ns, rows, ML_HV), ML_DV, ns, rows)
    xn = xn_ref[...].reshape(ns * rows, D_MODEL)
    gates = jnp.dot(xn, wg_ref[0], preferred_element_type=F32).reshape(ns, rows, LANES) + gb_ref[...]
    i_col = _heads(gates[:, :, :ML_HEADS], 1, ns, rows)
    f_col = _heads(jax.nn.log_sigmoid(gates[:, :, ML_HEADS:2 * ML_HEADS]), 1, ns, rows)

    t_idx = lax.broadcasted_iota(jnp.int32, (rows, rows), 0)
    s_idx = lax.broadcasted_iota(jnp.int32, (rows, rows), 1)
    eye = t_idx == s_idx
    causal = t_idx >= s_idx
    i_row = jnp.sum(jnp.where(eye, i_col, 0.0), axis=1, keepdims=True)
    f_row = jnp.sum(jnp.where(eye, f_col, 0.0), axis=1, keepdims=True)
    b_col = jnp.sum(jnp.where(causal, f_row, 0.0), axis=2, keepdims=True)
    b_row = jnp.sum(jnp.where(t_idx <= s_idx, f_col, 0.0), axis=1, keepdims=True)
    dmat = jnp.where(causal, b_col - b_row + i_row, -jnp.inf)

    m_prev = m_ref[...].reshape(G, 1, LANES)[:, :, 0:1]
    inter = b_col + m_prev
    m_t = jnp.maximum(inter, jnp.max(dmat, axis=2, keepdims=True))
    qb, kb, vb = q.astype(BF16), k.astype(BF16), v.astype(BF16)
    s = jnp.einsum("gld,gsd->gls", qb, kb, preferred_element_type=F32) * jnp.exp(dmat - m_t)
    w_inter = jnp.exp(inter - m_t)
    c_old = c_ref[...].reshape(G, ML_DK, ML_DV)
    n_old = n_ref[...].reshape(G, 1, ML_DK)
    num = (w_inter * jnp.einsum("gld,gde->gle", qb, c_old.astype(BF16), preferred_element_type=F32)
           + jnp.einsum("gls,gse->gle", s.astype(BF16), vb, preferred_element_type=F32))
    den = w_inter * jnp.sum(q * n_old, axis=2, keepdims=True) + jnp.sum(s, axis=2, keepdims=True)
    h = num / jnp.maximum(jnp.abs(den), jnp.exp(-m_t))

    m_new = m_t[:, rows - 1:rows, :]
    b_last = b_col[:, rows - 1:rows, :]
    w_state = jnp.exp(b_last - b_col + i_col - m_new)
    decay = jnp.exp(b_last + m_prev - m_new)
    kw = k * w_state
    kwt = jnp.swapaxes(kw, 1, 2).astype(BF16)
    c_new = decay * c_old + jnp.einsum("gds,gse->gde", kwt, vb, preferred_element_type=F32)
    n_new = decay * n_old + jnp.sum(kw, axis=1, keepdims=True)
    c_ref[...] = c_new.reshape(ns, ML_HEADS, ML_DK, ML_DV)
    n_ref[...] = n_new.reshape(ns, ML_HEADS, 1, ML_DK)
    m_ref[...] = jnp.broadcast_to(m_new, (G, 1, LANES)).reshape(ns, ML_HEADS, 1, LANES)

    hn = h * lax.rsqrt(jnp.mean(h * h, axis=2, keepdims=True) + EPS)
    hn = hn.reshape(ns, ML_HEADS, rows, ML_DV)
    o3 = o_ref[...].reshape(ns, rows, ML_HV)
    for hh in range(ML_HEADS):
        sl = slice(hh * ML_DV, (hh + 1) * ML_DV)
        y = hn[:, hh] * ng_ref[:, sl] * jax.nn.sigmoid(o3[:, :, sl])
        out_ref[..., sl] = y.reshape(out_ref.shape[:-1] + (ML_DV,)).astype(out_ref.dtype)


def _stacked_out_spec(n_layers, layer, blk, idx):
    if layer == 0:
        return pl.BlockSpec((n_layers,) + blk, lambda *g: (0,) + idx(*g))
    return pl.BlockSpec((None,) + blk, lambda *g: (layer,) + idx(*g))


def mlstm(proj, xn, w_gate, cw, cb, gbias, ng, buf, c0, layer, n0, m0, c_prev, seq, rows, ns):
    m = proj.shape[0]
    nb = m // seq
    n_layers = c0.shape[0]
    tps = seq // rows
    tm = ns * rows
    w = 2 * ML_HK
    kern = functools.partial(_mlstm_kernel, ns=ns, rows=rows, tps=tps, layer=layer)
    assert ns == 1 or rows == seq
    rows_blk = lambda width, col: pl.BlockSpec((tm, width), lambda b, t: (b * tps + t, col))
    row_blk = lambda col: rows_blk(w, col)
    st4 = lambda *shape: pl.BlockSpec((ns,) + shape, lambda b, t: (b, 0, 0, 0))
    c_blk = (ns, ML_HEADS, ML_DK, ML_DV)
    in_specs = [
        row_blk(D_RG // w),
        row_blk((CONV_W + D_RG) // w),
        row_blk((CONV_W + D_RG + ML_HV) // w),
        rows_blk(D_MODEL, 0),
        pl.BlockSpec((1, D_MODEL, LANES), lambda b, t: (layer, 0, 0)),
        pl.BlockSpec((CONV_K, w), lambda b, t: (0, D_RG // w)),
        pl.BlockSpec((1, w), lambda b, t: (0, D_RG // w)),
        pl.BlockSpec((1, LANES), lambda b, t: (0, 0)),
        pl.BlockSpec((1, ML_HV), lambda b, t: (0, 0)),
        pl.BlockSpec((None, ns, CONV_K - 1, w), lambda b, t: (layer, b, 0, D_RG // w)),
        pl.BlockSpec((None,) + c_blk, lambda b, t: (layer, b, 0, 0, 0)),
        st4(ML_HEADS, 1, ML_DK),
        st4(ML_HEADS, 1, LANES),
    ]
    args = [proj, proj, proj, xn, w_gate, cw, cb, gbias, ng, buf, c0, n0, m0]
    aliases = {}
    if layer > 0:
        in_specs.append(pl.BlockSpec(memory_space=pl.ANY))
        args.append(c_prev)
        aliases = {len(args) - 1: 2}
    return pl.pallas_call(
        kern,
        grid=(nb // ns, tps),
        in_specs=in_specs,
        out_specs=[
            rows_blk(ML_HV, 0),
            pl.BlockSpec((ns, CONV_K - 1, w), lambda b, t: (b, 0, 0)),
            _stacked_out_spec(n_layers, layer, c_blk, lambda b, t: (b, 0, 0, 0)),
            st4(ML_HEADS, 1, ML_DK),
            st4(ML_HEADS, 1, LANES),
        ],
        out_shape=[
            jax.ShapeDtypeStruct((m, ML_HV), BF16),
            jax.ShapeDtypeStruct((nb, CONV_K - 1, w), F32),
            jax.ShapeDtypeStruct((n_layers, nb) + c_blk[1:], F32),
            jax.ShapeDtypeStruct((nb, ML_HEADS, 1, ML_DK), F32),
            jax.ShapeDtypeStruct((nb, ML_HEADS, 1, LANES), F32),
        ],
        scratch_shapes=[pltpu.VMEM((ns, HALO + rows, w), F32)],
        input_output_aliases=aliases,
        compiler_params=_params("parallel", "arbitrary"),
        name="mlstm",
    )(*args)


def _layernorm(v, g_ref, b_ref):
    mu = jnp.mean(v, axis=-1, keepdims=True)
    vc = v - mu
    var = jnp.mean(vc * vc, axis=-1, keepdims=True)
    return vc * lax.rsqrt(var + EPS) * g_ref[...] + b_ref[...]


def _odd_in_proj(a, w_ref):
    u = jax.nn.gelu(jnp.dot(a, w_ref[0, :, 0:D_C], preferred_element_type=F32))
    v = jax.nn.gelu(jnp.dot(a, w_ref[0, :, D_C:2 * D_C], preferred_element_type=F32))
    return u, v


def _odd_long_kernel(a_ref, w_ref, lg_ref, lb_ref, ws_ref, sb_ref, out_ref):
    n = a_ref.shape[0] // C_CHUNK
    nxt = _odd_in_proj(a_ref[0:C_CHUNK, :], w_ref)
    for c in range(n):
        rs = slice(c * C_CHUNK, (c + 1) * C_CHUNK)
        u, v = nxt
        if c + 1 < n:
            nxt = _odd_in_proj(a_ref[(c + 1) * C_CHUNK:(c + 2) * C_CHUNK, :], w_ref)
        vn = _layernorm(v, lg_ref, lb_ref).astype(BF16)
        for g in range(C_GROUPS):
            sl = slice(g * C_GW, (g + 1) * C_GW)
            mix = jnp.dot(ws_ref[0, g], vn[:, sl], preferred_element_type=F32) + sb_ref[:, g:g + 1]
            out_ref[rs, sl] = (u[:, sl] * mix).astype(out_ref.dtype)


def _odd_short_kernel(a_ref, w_ref, lg_ref, lb_ref, we_ref, be_ref, out_ref, vn_ref, *, rows):
    rc = C_CHUNK
    nsc = rc // rows
    for c in range(a_ref.shape[0] // rc):
        rs = slice(c * rc, (c + 1) * rc)
        u, v = _odd_in_proj(a_ref[rs, :], w_ref)
        vn = _layernorm(v, lg_ref, lb_ref)
        vn_ref[rs, :] = vn
        v3 = vn.reshape(nsc, rows, D_C)
        mix = jnp.broadcast_to(be_ref[...], (nsc, rows, D_C))
        for s in range(rows):
            mix = mix + we_ref[s] * v3[:, s:s + 1, :]
        out_ref[rs, :] = (u * mix.reshape(rc, D_C)).astype(out_ref.dtype)


def odd_mixer(xn, w, layer, lg, lb, mix_w, mix_b, rows=None, tm=512):
    m, d = xn.shape
    tm = min(tm, m)
    row = lambda width: pl.BlockSpec((tm, width), lambda i: (i, 0))
    vec = pl.BlockSpec((1, D_C), lambda i: (0, 0))
    in_specs = [row(d), pl.BlockSpec((1, d, 2 * D_C), lambda i: (layer, 0, 0), pipeline_mode=pl.Buffered(1)), vec, vec]
    if rows is None:
        kern = _odd_long_kernel
        in_specs += [pl.BlockSpec((1, C_GROUPS, C_CHUNK, C_CHUNK), lambda i: (layer, 0, 0, 0)),
                     pl.BlockSpec((C_CHUNK, C_GROUPS), lambda i: (0, 0))]
        out_specs = [row(D_C)]
        out_shape = [jax.ShapeDtypeStruct((m, D_C), BF16)]
    else:
        kern = functools.partial(_odd_short_kernel, rows=rows)
        in_specs += [pl.BlockSpec((rows, rows, D_C), lambda i: (0, 0, 0)), pl.BlockSpec((rows, D_C), lambda i: (0, 0))]
        out_specs = [row(D_C), row(D_C)]
        out_shape = [jax.ShapeDtypeStruct((m, D_C), BF16), jax.ShapeDtypeStruct((m, D_C), F32)]
    return pl.pallas_call(
        kern,
        grid=(m // tm,),
        in_specs=in_specs,
        out_specs=out_specs,
        out_shape=out_shape,
        compiler_params=_params("arbitrary"),
        name="odd_mixer",
    )(xn, w, lg, lb, mix_w, mix_b)


def _block_diag(w):
    per = RG_CB // RG_BW
    w = w.reshape(D_RG // RG_CB, per, RG_BW, RG_BW)
    eye = jnp.eye(per, dtype=w.dtype)
    bd = w[:, :, :, None, :] * eye[None, :, None, :, None]
    return bd.reshape(D_RG // RG_CB, RG_CB, RG_CB).astype(BF16)


def _prep_weights(P):
    n_even = P["w_in_even"].shape[0]
    pad = LANES - 2 * ML_HEADS
    return {
        "in_even": P["w_in_even"].astype(BF16),
        "in_gate": jnp.pad(P["w_in_even"][:, :, E_MAIN:], ((0, 0), (0, 0), (0, pad))).astype(BF16),
        "wa": [_block_diag(P["rg_wa"][j]) for j in range(n_even)],
        "wx": [_block_diag(P["rg_wx"][j]) for j in range(n_even)],
        "gbias": [jnp.pad(P["ml_gate_b"][j].reshape(1, 2 * ML_HEADS), ((0, 0), (0, pad))) for j in range(n_even)],
        "in_odd": P["w_in_odd"].astype(BF16),
        "sgu_ws": jnp.tril(P["sgu_ws"]).astype(BF16),
        "down": P["ffn_w_down"].astype(BF16),
    }


def _sgu_short_weights(ws, sb, rows):
    w = jnp.tril(ws)[:, :rows, :rows]
    w_exp = jnp.repeat(jnp.transpose(w, (2, 1, 0)), C_GW, axis=2)
    b_exp = jnp.repeat(jnp.transpose(sb[:, :rows]), C_GW, axis=1)
    return w_exp, b_exp


def _trunk(x, conv_buf, rg_h, ml_c, ml_n, ml_m, ffn_buf, P, W, short):
    nb, seq, d = x.shape
    m = nb * seq
    depth = P["ffn_w_up"].shape[0]
    x = x.reshape(m, d)
    if short:
        rg_rows, rg_ns = seq, 32
        ml_rows, ml_ns = seq, 8
    else:
        rg_rows, rg_ns = 256, 1
        ml_rows, ml_ns = 128, 1
    conv_l, h_l, n_l, m_l, v_l = [], [], [], [], []
    c_all = f_all = None
    xn = rmsnorm(x, P["norm_mix"][0], BF16)
    for layer in range(depth):
        j = layer // 2
        g_ffn = P["norm_ffn"][layer]
        if layer % 2 == 0:
            proj = matmul_nt(xn, W["in_even"], j, E_MAIN)
            cw, cb = P["conv_even_w"][j], P["conv_even_b"][j].reshape(1, CONV_W)
            rg_out, cst_rg, h_last = rglru(
                proj, cw, cb, W["wa"][j], W["wx"][j], P["rg_ba"][j].reshape(1, D_RG), P["rg_bx"][j].reshape(1, D_RG),
                P["rg_lambda"][j].reshape(1, D_RG), conv_buf, j, rg_h[j].reshape(nb, 1, D_RG), seq, rg_rows, rg_ns)
            ml_out, cst_ml, c_all, n_new, m_new = mlstm(
                proj, xn, W["in_gate"], cw, cb, W["gbias"][j], P["ml_norm_g"][j].reshape(1, ML_HV), conv_buf, ml_c, j,
                ml_n[j].reshape(nb, ML_HEADS, 1, ML_DK),
                jnp.broadcast_to(ml_m[j][:, :, None, None], (nb, ML_HEADS, 1, LANES)), c_all, seq, ml_rows, ml_ns)
            x, xn = matmul_res_norm([rg_out, ml_out], P["w_out_even"], j, x, g=g_ffn)
            conv_l.append(jnp.concatenate([cst_rg, cst_ml], axis=-1))
            h_l.append(h_last.reshape(nb, D_RG))
            n_l.append(n_new.reshape(nb, ML_HEADS, ML_DK))
            m_l.append(m_new[:, :, 0, 0])
        else:
            lg, lb = P["sgu_ln_g"][j].reshape(1, D_C), P["sgu_ln_b"][j].reshape(1, D_C)
            if short:
                w_exp, b_exp = _sgu_short_weights(P["sgu_ws"][j], P["sgu_b"][j], seq)
                gated, vn = odd_mixer(xn, W["in_odd"], j, lg, lb, w_exp, b_exp, rows=seq)
                v_l.append(vn.reshape(nb, seq, D_C))
            else:
                gated, = odd_mixer(xn, W["in_odd"], j, lg, lb, W["sgu_ws"], jnp.transpose(P["sgu_b"][j]))
            x, xn = matmul_res_norm([gated], P["w_out_odd"], j, x, g=g_ffn)
        hid, f_all = ffn_up(xn, P["ffn_w_up"], layer, P["ffn_conv_w"][layer], P["ffn_conv_b"][layer].reshape(1, D_FF),
                            ffn_buf, f_all, seq)
        last = layer == depth - 1
        g_next = P["norm_final"] if last else P["norm_mix"][layer + 1]
        x, xn = matmul_res_norm([hid], W["down"], layer, x, g=g_next, xn_dtype=F32 if last else BF16,
                                emit_x=not last, tm=256)
    y = xn.reshape(nb, seq, d)
    return y, jnp.stack(conv_l), jnp.stack(h_l), c_all, jnp.stack(n_l), jnp.stack(m_l), v_l, f_all


def kernel(x_prompt, x_sample, state_conv_mix, state_rglru_h, state_mlstm_C, state_mlstm_n, state_mlstm_m, state_ffn_conv, norm_mix, norm_ffn, norm_final, w_in_even, conv_even_w, conv_even_b, rg_wa, rg_ba, rg_wx, rg_bx, rg_lambda, ml_gate_b, ml_norm_g, w_out_even, w_in_odd, sgu_ln_g, sgu_ln_b, sgu_ws, sgu_b, w_out_odd, ffn_w_up, ffn_conv_w, ffn_conv_b, ffn_w_down):
    P = {
        "norm_mix": norm_mix, "norm_ffn": norm_ffn, "norm_final": norm_final,
        "w_in_even": w_in_even, "conv_even_w": conv_even_w, "conv_even_b": conv_even_b,
        "rg_wa": rg_wa, "rg_ba": rg_ba, "rg_wx": rg_wx, "rg_bx": rg_bx, "rg_lambda": rg_lambda,
        "ml_gate_b": ml_gate_b, "ml_norm_g": ml_norm_g, "w_out_even": w_out_even,
        "w_in_odd": w_in_odd, "sgu_ln_g": sgu_ln_g, "sgu_ln_b": sgu_ln_b, "sgu_ws": sgu_ws,
        "sgu_b": sgu_b, "w_out_odd": w_out_odd, "ffn_w_up": ffn_w_up, "ffn_conv_w": ffn_conv_w,
        "ffn_conv_b": ffn_conv_b, "ffn_w_down": ffn_w_down,
    }
    W = _prep_weights(P)
    n_even, depth = w_in_even.shape[0], ffn_w_up.shape[0]
    bp = x_prompt.shape[0]
    dt = x_prompt.dtype
    z_conv = jnp.zeros((n_even, bp, CONV_K - 1, CONV_W), dt)
    z_h = jnp.zeros((n_even, bp, D_RG), dt)
    z_c = jnp.zeros((n_even, bp, ML_HEADS, ML_DK, ML_DV), dt)
    z_n = jnp.zeros((n_even, bp, ML_HEADS, ML_DK), dt)
    z_m = jnp.zeros((n_even, bp, ML_HEADS), dt)
    z_f = jnp.zeros((depth, bp, FFN_K - 1, D_FF), dt)
    y_p, conv_p, h_p, c_p, n_p, m_p, _, f_p = _trunk(x_prompt, z_conv, z_h, z_c, z_n, z_m, z_f, P, W, short=False)
    y_s, conv_s, h_s, c_s, n_s, m_s, v_list, f_s = _trunk(
        x_sample, state_conv_mix, state_rglru_h, state_mlstm_C, state_mlstm_n, state_mlstm_m, state_ffn_conv,
        P, W, short=True)
    chunk_v_s = jnp.stack(v_list)
    return (y_p, y_s, conv_p, conv_s, h_p, h_s, c_p, c_s, n_p, n_s, m_p, m_s, chunk_v_s, f_p, f_s)
```
